```python
import jax, jax.numpy as jnp
from jax import lax
import numpy as np

D_MODEL = 1024
BATCH = 4
SEQ = 4096
DEPTH = 2
DEC_BATCH = 32
DEC_SEQ = 1
PAST_LEN = 8192
PAGE_SIZE = 128

HEAD_DIM = 64
R_HEADS = 8
R_WIDTH = R_HEADS * HEAD_DIM
DECAY_LORA = 64
ICLR_LORA = 64
GATE_LORA = 128
R_COLS = 3 * R_WIDTH + DECAY_LORA + ICLR_LORA + GATE_LORA
LNX_EPS = 64e-5
CONV_CH = 512
CONV_K = 31
SWA_GROUPS = ((128, 1), (512, 4), (2048, 16))
N_GROUPS = len(SWA_GROUPS)
G_HEADS = 4
A_WIDTH = N_GROUPS * G_HEADS * HEAD_DIM
Q_BLOCK = 128
ATTN_SCALE = HEAD_DIM ** -0.5
N_BRANCH = 3
D_FF = 2816
FFN_K = 3
OFF_R = 0
OFF_C = OFF_R + R_COLS
OFF_Q = OFF_C + 2 * CONV_CH
OFF_K = OFF_Q + A_WIDTH
OFF_V = OFF_K + A_WIDTH
OFF_GATE = OFF_V + A_WIDTH
IN_COLS = OFF_GATE + N_BRANCH * D_MODEL
ALPHA = (2 * DEPTH) ** 0.25
BETA = (8 * DEPTH) ** -0.25
LN_EPS = 1e-5

kernel_name = 'hybrid_rwkv7_conformer_dilated_swa_step'


def layer_norm(x, g, b, eps=LN_EPS):
    xf = x.astype(jnp.float32)
    mu = xf.mean(-1, keepdims=True)
    var = jnp.square(xf - mu).mean(-1, keepdims=True)
    y = (xf - mu) * lax.rsqrt(var + eps) * g.astype(jnp.float32) + b.astype(jnp.float32)
    return y.astype(x.dtype)


def causal_dwconv(u, buf, w, b):
    full = jnp.concatenate([buf.astype(u.dtype), u], axis=1)
    y = lax.conv_general_dilated(full, w[:, None, :].astype(u.dtype), window_strides=(1,),
                                 padding='VALID', dimension_numbers=('NWC', 'WIO', 'NWC'),
                                 feature_group_count=u.shape[-1])
    return y + b, full[:, -(w.shape[0] - 1):]


def rwkv7_mix(zr, shift_prev, wkv0, mu, w0, w_up, a0, a_up, g_up, k_k, k_a, r_k, ln_g, ln_b, w_out):
    f32 = jnp.float32
    B, L, _ = zr.shape
    prev = jnp.concatenate([shift_prev.astype(zr.dtype), zr[:, :-1]], axis=1)
    zs = zr + (prev - zr) * mu
    cut = [R_WIDTH, 2 * R_WIDTH, 3 * R_WIDTH, 3 * R_WIDTH + DECAY_LORA, 3 * R_WIDTH + DECAY_LORA + ICLR_LORA]
    r, k, v, wd, ad, gd = jnp.split(zs, cut, axis=-1)
    w_log = -jax.nn.softplus(-(w0 + jnp.tanh(wd) @ w_up).astype(f32)) - 0.5
    decay = jnp.exp(-jnp.exp(w_log))
    a = jax.nn.sigmoid((a0 + ad @ a_up).astype(f32))
    g = jax.nn.sigmoid(gd) @ g_up
    heads = lambda t: t.astype(f32).reshape(B, L, R_HEADS, HEAD_DIM)
    r, k, v, decay, a = heads(r), heads(k), heads(v), heads(decay), heads(a)
    kk = k * k_k.astype(f32).reshape(R_HEADS, HEAD_DIM)
    kk = kk / jnp.maximum(jnp.sqrt(jnp.sum(kk * kk, axis=-1, keepdims=True)), 1e-12)
    k = k * (1.0 + (a - 1.0) * k_a.astype(f32).reshape(R_HEADS, HEAD_DIM))

    def step(S, inp):
        r_t, w_t, k_t, v_t, kk_t, b_t = inp
        sa = jnp.einsum('bhvk,bhk->bhv', S, -kk_t)
        S = S * w_t[:, :, None, :] + sa[..., None] * b_t[:, :, None, :] + v_t[..., None] * k_t[:, :, None, :]
        return S, jnp.einsum('bhvk,bhk->bhv', S, r_t)

    tm = lambda t: jnp.swapaxes(t, 0, 1)
    S_fin, o = lax.scan(step, wkv0.astype(f32), (tm(r), tm(decay), tm(k), tm(v), tm(kk), tm(kk * a)))
    o = tm(o)
    m = o.mean(-1, keepdims=True)
    var = jnp.square(o - m).mean(-1, keepdims=True)
    o = ((o - m) * lax.rsqrt(var + LNX_EPS)).reshape(B, L, R_WIDTH) * ln_g.astype(f32) + ln_b.astype(f32)
    bonus = jnp.sum(r * k * r_k.astype(f32), axis=-1, keepdims=True) * v
    o = (o + bonus.reshape(B, L, R_WIDTH)).astype(zr.dtype) * g
    return o @ w_out, zr[:, -1:], S_fin


def conformer_conv(zc, buf, dw, dw_b, ln_g, ln_b, w_out):
    u = zc[..., :CONV_CH] * jax.nn.sigmoid(zc[..., CONV_CH:])
    c, new_buf = causal_dwconv(u, buf, dw, dw_b)
    c = layer_norm(c, ln_g, ln_b)
    c = c * jax.nn.sigmoid(c)
    return c @ w_out, new_buf


def dilated_attn_prompt(q, k, v, dil, nk):
    B, S, H, E = q.shape
    L = S // dil
    nb = -(-L // Q_BLOCK)
    Lp = nb * Q_BLOCK

    def split(t):
        t = t.reshape(B, L, dil, H, E).transpose(0, 2, 1, 3, 4)
        t = jnp.pad(t, ((0, 0), (0, 0), (0, Lp - L), (0, 0), (0, 0)))
        return t.reshape(B, dil, nb, Q_BLOCK, H, E)

    def with_prev(t):
        prev = jnp.pad(t, ((0, 0), (0, 0), (1, 0), (0, 0), (0, 0), (0, 0)))[:, :, :-1]
        return jnp.concatenate([prev, t], axis=3)

    qb = split(q)
    kb, vb = with_prev(split(k)), with_prev(split(v))
    s = jnp.einsum('bdnqhe,bdnkhe->bdnhqk', qb, kb).astype(jnp.float32)
    qi = jnp.arange(Q_BLOCK)[:, None]
    ki = jnp.arange(2 * Q_BLOCK)[None, :]
    dist = qi + Q_BLOCK - ki
    keypos = jnp.arange(nb)[:, None, None] * Q_BLOCK + ki[None] - Q_BLOCK
    mask = (dist >= 0)[None] & (dist <= nk)[None] & (keypos >= 0)
    s = jnp.where(mask[None, None, :, None], s, -jnp.inf)
    lse = jax.nn.logsumexp(s, axis=-1)
    p = jnp.exp(s - lse[..., None])
    o = jnp.einsum('bdnhqk,bdnkhe->bdnqhe', p.astype(v.dtype), vb)
    o = o.reshape(B, dil, Lp, H, E)[:, :, :L].transpose(0, 2, 1, 3, 4).reshape(B, S, H, E)
    lse = lse.transpose(0, 1, 2, 4, 3).reshape(B, dil, Lp, H)[:, :, :L].transpose(0, 2, 1, 3).reshape(B, S, H)
    return o, lse


def dilated_attn_sample(q, k_all, v_all, n_buf, dil, nk):
    T = q.shape[1]
    idx = n_buf + jnp.arange(T)[:, None] - dil * jnp.arange(nk + 1)[None, :]
    valid = idx >= 0
    idx = jnp.maximum(idx, 0)
    kg, vg = k_all[:, idx], v_all[:, idx]
    s = jnp.einsum('bthe,btjhe->bthj', q, kg).astype(jnp.float32)
    s = jnp.where(valid[None, :, None, :], s, -jnp.inf)
    lse = jax.nn.logsumexp(s, axis=-1)
    p = jnp.exp(s - lse[..., None])
    o = jnp.einsum('bthj,btjhe->bthe', p.astype(v_all.dtype), vg)
    return o, lse


def decoder_layer(x, prm, shift, wkv, conv_buf, ffn_buf, kv_bufs):
    B, L, _ = x.shape
    z = x @ prm['w_in']
    zr = z[..., OFF_R:OFF_C]
    zc = z[..., OFF_C:OFF_Q]
    q = z[..., OFF_Q:OFF_K].reshape(B, L, N_GROUPS, G_HEADS, HEAD_DIM) * ATTN_SCALE
    k = z[..., OFF_K:OFF_V].reshape(B, L, N_GROUPS, G_HEADS, HEAD_DIM)
    v = z[..., OFF_V:OFF_GATE].reshape(B, L, N_GROUPS, G_HEADS, HEAD_DIM)
    zg = z[..., OFF_GATE:].reshape(B, L, N_BRANCH, D_MODEL)

    y_r, new_shift, new_wkv = rwkv7_mix(zr, shift, wkv, prm['rwkv_mu'], prm['rwkv_w0'], prm['rwkv_w_up'],
                                        prm['rwkv_a0'], prm['rwkv_a_up'], prm['rwkv_g_up'], prm['rwkv_k_k'],
                                        prm['rwkv_k_a'], prm['rwkv_r_k'], prm['rwkv_ln_g'], prm['rwkv_ln_b'],
                                        prm['w_rwkv_out'])
    y_c, new_conv = conformer_conv(zc, conv_buf, prm['conv_dw'], prm['conv_dw_b'], prm['conv_ln_g'],
                                   prm['conv_ln_b'], prm['w_conv_out'])

    outs, lses, new_kv = [], [], []
    for gi, (win, dil) in enumerate(SWA_GROUPS):
        qg, kg, vg = q[:, :, gi], k[:, :, gi], v[:, :, gi]
        if kv_bufs is None:
            o, lse = dilated_attn_prompt(qg, kg, vg, dil, win // dil)
            keep = min(win, L)
            new_kv.append(jnp.stack([kg[:, L - keep:], vg[:, L - keep:]], axis=2))
        else:
            buf = kv_bufs[gi]
            n_buf = buf.shape[1]
            k_all = jnp.concatenate([buf[:, :, 0].astype(kg.dtype), kg], axis=1)
            v_all = jnp.concatenate([buf[:, :, 1].astype(vg.dtype), vg], axis=1)
            o, lse = dilated_attn_sample(qg, k_all, v_all, n_buf, dil, win // dil)
            keep = min(win, n_buf + L)
            new_kv.append(jnp.stack([k_all[:, -keep:], v_all[:, -keep:]], axis=2))
        outs.append(o)
        lses.append(lse)
    wts = jax.nn.softmax(jnp.stack(lses, axis=2), axis=2)
    o = jnp.sum(wts[..., None] * jnp.stack(outs, axis=2).astype(jnp.float32), axis=2)
    y_a = o.astype(x.dtype).reshape(B, L, G_HEADS * HEAD_DIM) @ prm['w_attn_out']

    gates = jax.nn.sigmoid(zg + prm['gate_b'])
    merged = gates[:, :, 0] * y_r + gates[:, :, 1] * y_c + gates[:, :, 2] * y_a
    h = layer_norm(ALPHA * x + merged @ prm['w_o'], prm['ln1_g'], prm['ln1_b'])

    u, new_ffn = causal_dwconv(h @ prm['w_ffn_in'], ffn_buf, prm['ffn_dw'], prm['ffn_dw_b'])
    f = (jax.nn.silu(u[..., :D_FF]) * u[..., D_FF:]) @ prm['w_ffn_out']
    out = layer_norm(ALPHA * h + f, prm['ln2_g'], prm['ln2_b'])
    return out, new_shift, new_wkv, new_conv, new_kv, new_ffn


def setup_inputs(seed: int = 0) -> dict:
    key = jax.random.key(seed)
    ks = iter(jax.random.split(key, 40))
    f32 = jnp.float32

    def nrm(shape, scale=1.0):
        return scale * jax.random.normal(next(ks), shape, f32)

    def unif(shape, lo, hi):
        return jax.random.uniform(next(ks), shape, f32, lo, hi)

    kv_len = [min(win, PAST_LEN) for win, _ in SWA_GROUPS]
    return {
        'x_prompt': nrm((BATCH, SEQ, D_MODEL)),
        'x_sample': nrm((DEC_BATCH, DEC_SEQ, D_MODEL)),
        'state_shift': nrm((DEPTH, DEC_BATCH, 1, R_COLS)),
        'state_wkv': nrm((DEPTH, DEC_BATCH, R_HEADS, HEAD_DIM, HEAD_DIM), 0.5),
        'state_conv': nrm((DEPTH, DEC_BATCH, CONV_K - 1, CONV_CH), 0.5),
        'cache_swa_a': nrm((DEPTH, DEC_BATCH, kv_len[0], 2, G_HEADS, HEAD_DIM)),
        'cache_swa_b': nrm((DEPTH, DEC_BATCH, kv_len[1], 2, G_HEADS, HEAD_DIM)),
        'cache_swa_c': nrm((DEPTH, DEC_BATCH, kv_len[2], 2, G_HEADS, HEAD_DIM)),
        'state_ffn': nrm((DEPTH, DEC_BATCH, FFN_K - 1, 2 * D_FF)),
        'w_in': nrm((DEPTH, D_MODEL, IN_COLS), D_MODEL ** -0.5),
        'rwkv_mu': unif((DEPTH, R_COLS), 0.0, 1.0),
        'rwkv_w0': unif((DEPTH, R_WIDTH), -3.0, 1.0),
        'rwkv_w_up': nrm((DEPTH, DECAY_LORA, R_WIDTH), 0.1),
        'rwkv_a0': nrm((DEPTH, R_WIDTH), 0.1),
        'rwkv_a_up': nrm((DEPTH, ICLR_LORA, R_WIDTH), 0.1),
        'rwkv_g_up': nrm((DEPTH, GATE_LORA, R_WIDTH), GATE_LORA ** -0.5),
        'rwkv_k_k': 0.85 + nrm((DEPTH, R_WIDTH), 0.05),
        'rwkv_k_a': 1.0 + nrm((DEPTH, R_WIDTH), 0.05),
        'rwkv_r_k': nrm((DEPTH, R_HEADS, HEAD_DIM), 0.1),
        'rwkv_ln_g': 1.0 + nrm((DEPTH, R_WIDTH), 0.05),
        'rwkv_ln_b': nrm((DEPTH, R_WIDTH), 0.02),
        'w_rwkv_out': nrm((DEPTH, R_WIDTH, D_MODEL), R_WIDTH ** -0.5),
        'conv_dw': nrm((DEPTH, CONV_K, CONV_CH), CONV_K ** -0.5),
        'conv_dw_b': nrm((DEPTH, CONV_CH), 0.02),
        'conv_ln_g': 1.0 + nrm((DEPTH, CONV_CH), 0.05),
        'conv_ln_b': nrm((DEPTH, CONV_CH), 0.02),
        'w_conv_out': nrm((DEPTH, CONV_CH, D_MODEL), CONV_CH ** -0.5),
        'w_attn_out': nrm((DEPTH, G_HEADS * HEAD_DIM, D_MODEL), (G_HEADS * HEAD_DIM) ** -0.5),
        'gate_b': nrm((DEPTH, N_BRANCH, D_MODEL), 0.02),
        'w_o': nrm((DEPTH, D_MODEL, D_MODEL), BETA * D_MODEL ** -0.5),
        'ln1_g': 1.0 + nrm((DEPTH, D_MODEL), 0.05),
        'ln1_b': nrm((DEPTH, D_MODEL), 0.02),
        'w_ffn_in': nrm((DEPTH, D_MODEL, 2 * D_FF), D_MODEL ** -0.5),
        'ffn_dw': nrm((DEPTH, FFN_K, 2 * D_FF), FFN_K ** -0.5),
        'ffn_dw_b': nrm((DEPTH, 2 * D_FF), 0.02),
        'w_ffn_out': nrm((DEPTH, D_FF, D_MODEL), BETA * D_FF ** -0.5),
        'ln2_g': 1.0 + nrm((DEPTH, D_MODEL), 0.05),
        'ln2_b': nrm((DEPTH, D_MODEL), 0.02),
    }


def reference(x_prompt, x_sample, state_shift, state_wkv, state_conv, cache_swa_a, cache_swa_b,
              cache_swa_c, state_ffn, w_in, rwkv_mu, rwkv_w0, rwkv_w_up, rwkv_a0, rwkv_a_up, rwkv_g_up,
              rwkv_k_k, rwkv_k_a, rwkv_r_k, rwkv_ln_g, rwkv_ln_b, w_rwkv_out, conv_dw, conv_dw_b,
              conv_ln_g, conv_ln_b, w_conv_out, w_attn_out, gate_b, w_o, ln1_g, ln1_b, w_ffn_in,
              ffn_dw, ffn_dw_b, w_ffn_out, ln2_g, ln2_b):
    hp, hs = x_prompt, x_sample
    bp = x_prompt.shape[0]
    sh_p, sh_s, wk_p, wk_s, cv_p, cv_s, ff_p, ff_s = [], [], [], [], [], [], [], []
    sa_p, sa_s, sb_p, sb_s, sc_p, sc_s = [], [], [], [], [], []
    for l in range(DEPTH):
        prm = {
            'w_in': w_in[l], 'rwkv_mu': rwkv_mu[l], 'rwkv_w0': rwkv_w0[l], 'rwkv_w_up': rwkv_w_up[l],
            'rwkv_a0': rwkv_a0[l], 'rwkv_a_up': rwkv_a_up[l], 'rwkv_g_up': rwkv_g_up[l],
            'rwkv_k_k': rwkv_k_k[l], 'rwkv_k_a': rwkv_k_a[l], 'rwkv_r_k': rwkv_r_k[l],
            'rwkv_ln_g': rwkv_ln_g[l], 'rwkv_ln_b': rwkv_ln_b[l], 'w_rwkv_out': w_rwkv_out[l],
            'conv_dw': conv_dw[l], 'conv_dw_b': conv_dw_b[l], 'conv_ln_g': conv_ln_g[l],
            'conv_ln_b': conv_ln_b[l], 'w_conv_out': w_conv_out[l], 'w_attn_out': w_attn_out[l],
            'gate_b': gate_b[l], 'w_o': w_o[l], 'ln1_g': ln1_g[l], 'ln1_b': ln1_b[l],
            'w_ffn_in': w_ffn_in[l], 'ffn_dw': ffn_dw[l], 'ffn_dw_b': ffn_dw_b[l],
            'w_ffn_out': w_ffn_out[l], 'ln2_g': ln2_g[l], 'ln2_b': ln2_b[l],
        }
        hp, n_sh, n_wk, n_cv, n_kv, n_ff = decoder_layer(
            hp, prm,
            jnp.zeros((bp, 1, R_COLS), hp.dtype),
            jnp.zeros((bp, R_HEADS, HEAD_DIM, HEAD_DIM), jnp.float32),
            jnp.zeros((bp, CONV_K - 1, CONV_CH), hp.dtype),
            jnp.zeros((bp, FFN_K - 1, 2 * D_FF), hp.dtype),
            None)
        sh_p.append(n_sh); wk_p.append(n_wk); cv_p.append(n_cv); ff_p.append(n_ff)
        sa_p.append(n_kv[0]); sb_p.append(n_kv[1]); sc_p.append(n_kv[2])
        hs, n_sh, n_wk, n_cv, n_kv, n_ff = decoder_layer(
            hs, prm, state_shift[l], state_wkv[l], state_conv[l], state_ffn[l],
            (cache_swa_a[l], cache_swa_b[l], cache_swa_c[l]))
        sh_s.append(n_sh); wk_s.append(n_wk); cv_s.append(n_cv); ff_s.append(n_ff)
        sa_s.append(n_kv[0]); sb_s.append(n_kv[1]); sc_s.append(n_kv[2])
    y_prompt, y_sample = hp, hs
    new_shift_p, new_shift_s = jnp.stack(sh_p), jnp.stack(sh_s)
    new_wkv_p, new_wkv_s = jnp.stack(wk_p), jnp.stack(wk_s)
    new_conv_p, new_conv_s = jnp.stack(cv_p), jnp.stack(cv_s)
    new_swa_a_p, new_swa_a_s = jnp.stack(sa_p), jnp.stack(sa_s)
    new_swa_b_p, new_swa_b_s = jnp.stack(sb_p), jnp.stack(sb_s)
    new_swa_c_p, new_swa_c_s = jnp.stack(sc_p), jnp.stack(sc_s)
    new_ffn_p, new_ffn_s = jnp.stack(ff_p), jnp.stack(ff_s)
    return (y_prompt, y_sample, new_shift_p, new_shift_s, new_wkv_p, new_wkv_s, new_conv_p, new_conv_s,
            new_swa_a_p, new_swa_a_s, new_swa_b_p, new_swa_b_s, new_swa_c_p, new_swa_c_s, new_ffn_p, new_ffn_s)
```

```python
import functools

import jax
import jax.numpy as jnp
from jax import lax
from jax.experimental import pallas as pl
from jax.experimental.pallas import tpu as pltpu

F32 = jnp.float32
BF16 = jnp.bfloat16

HEAD_DIM = 64
R_HEADS = 8
R_WIDTH = R_HEADS * HEAD_DIM
DECAY_LORA = 64
ICLR_LORA = 64
GATE_LORA = 128
R_COLS = 3 * R_WIDTH + DECAY_LORA + ICLR_LORA + GATE_LORA
LNX_EPS = 64e-5
CONV_CH = 512
CONV_K = 31
SWA_GROUPS = ((128, 1), (512, 4), (2048, 16))
N_GROUPS = len(SWA_GROUPS)
G_HEADS = 4
G_WIDTH = G_HEADS * HEAD_DIM
A_WIDTH = N_GROUPS * G_WIDTH
Q_BLOCK = 128
ATTN_SCALE = HEAD_DIM ** -0.5
N_BRANCH = 3
FFN_K = 3
LN_EPS = 1e-5
NEG_BIG = -1e30

LANES = 128
CHUNK = 64
VMEM_LIMIT = 56 * 1024 * 1024


def _cparams(sem):
    return pltpu.CompilerParams(dimension_semantics=sem, vmem_limit_bytes=VMEM_LIMIT)


def _dot(a, b):
    return jnp.dot(a.astype(BF16), b.astype(BF16), preferred_element_type=F32)


def _dot_nt(a, b):
    return lax.dot_general(a.astype(BF16), b.astype(BF16), (((1,), (1,)), ((), ())),
                           preferred_element_type=F32)


def _dot_tn(a, b):
    return lax.dot_general(a.astype(BF16), b.astype(BF16), (((0,), (0,)), ((), ())),
                           preferred_element_type=F32)


def _split2(x):
    hi = x.astype(BF16)
    lo = (x - hi.astype(F32)).astype(BF16)
    return hi, lo


def _split3(x):
    h1 = x.astype(BF16)
    r1 = x - h1.astype(F32)
    h2 = r1.astype(BF16)
    h3 = (r1 - h2.astype(F32)).astype(BF16)
    return h1, h2, h3


def _dot3(a, b):
    ah, al = _split2(a)
    bh, bl = _split2(b)
    d = lambda x, y: jnp.dot(x, y, preferred_element_type=F32)
    return d(ah, bh) + (d(ah, bl) + d(al, bh))


def _sigmoid(x):
    return 1.0 / (1.0 + jnp.exp(-x))


def _layer_norm(x, g, b):
    mu = jnp.mean(x, axis=-1, keepdims=True)
    xc = x - mu
    var = jnp.mean(xc * xc, axis=-1, keepdims=True)
    return xc * lax.rsqrt(var + LN_EPS) * g + b


def _iota(shape, dim):
    return lax.broadcasted_iota(jnp.int32, shape, dim)


def _head_ones():
    r = _iota((LANES, LANES), 0)
    c = _iota((LANES, LANES), 1)
    return jnp.where((r >> 6) == (c >> 6), 1.0, 0.0).astype(BF16)


def _segsum(x, ones):
    outs = []
    for p in range(x.shape[1] // LANES):
        hi, lo = _split2(x[:, p * LANES:(p + 1) * LANES])
        outs.append(jnp.dot(hi, ones, preferred_element_type=F32)
                    + jnp.dot(lo, ones, preferred_element_type=F32))
    return outs[0] if len(outs) == 1 else jnp.concatenate(outs, axis=1)


def _mm_kernel(x_ref, w_ref, o_ref):
    o_ref[...] = jnp.dot(x_ref[...].astype(BF16), w_ref[...],
                         preferred_element_type=F32).astype(o_ref.dtype)


def _mm(x, w, *, tm, tn, out_dtype=F32):
    M, K = x.shape
    N = w.shape[1]
    tm = min(tm, M)
    tn = min(tn, N)
    assert M % tm == 0 and N % tn == 0
    return pl.pallas_call(
        _mm_kernel,
        out_shape=jax.ShapeDtypeStruct((M, N), out_dtype),
        grid=(M // tm, N // tn),
        in_specs=[pl.BlockSpec((tm, K), lambda i, j: (i, 0)),
                  pl.BlockSpec((K, tn), lambda i, j: (0, j))],
        out_specs=pl.BlockSpec((tm, tn), lambda i, j: (i, j)),
        compiler_params=_cparams(("parallel", "parallel")),
        name="mm",
    )(x, w)


def _rwkv_prep(zr, prevs, mu, w0, w_up, a0, a_up, g_up, k_k, k_a, ones):
    zs = zr + (prevs - zr) * mu
    r = zs[:, 0:R_WIDTH]
    k = zs[:, R_WIDTH:2 * R_WIDTH]
    v = zs[:, 2 * R_WIDTH:3 * R_WIDTH]
    o1 = 3 * R_WIDTH
    wd = zs[:, o1:o1 + DECAY_LORA]
    ad = zs[:, o1 + DECAY_LORA:o1 + DECAY_LORA + ICLR_LORA]
    gd = zs[:, o1 + DECAY_LORA + ICLR_LORA:R_COLS]
    nx = -(w0 + _dot(jnp.tanh(wd), w_up))
    softplus = jnp.maximum(nx, 0.0) + jnp.log(1.0 + jnp.exp(-jnp.abs(nx)))
    logdec = -jnp.exp(-softplus - 0.5)
    a = _sigmoid(a0 + _dot(ad, a_up))
    g = _dot(_sigmoid(gd), g_up)
    kk = k * k_k
    kk = kk / jnp.maximum(jnp.sqrt(_segsum(kk * kk, ones)), 1e-12)
    k2 = k * (1.0 + (a - 1.0) * k_a)
    return r, k2, v, kk, a, g, logdec


def _rwkv_post(o, r, k2, v, g, r_k, ln_g, ln_b, ones):
    inv = 1.0 / HEAD_DIM
    m = _segsum(o, ones) * inv
    oc = o - m
    var = _segsum(oc * oc, ones) * inv
    on = oc * lax.rsqrt(var + LNX_EPS) * ln_g + ln_b
    bonus = _segsum(r * k2 * r_k, ones) * v
    return (on + bonus) * g


def _chunk_cumsum(logdec, tri_bd):
    h1, h2, h3 = _split3(logdec)
    d = lambda x: jnp.dot(tri_bd, x, preferred_element_type=F32)
    return d(h1) + d(h2) + d(h3)


def _rwkv_chunk_tables(logdec, cum_all, r, k2, v, kk, a, c):
    sl = slice(c * CHUNK, (c + 1) * CHUNK)
    cum = cum_all[sl]
    ld = logdec[sl]
    cum_end = cum[CHUNK - 1:CHUNK, :]
    p_in = jnp.exp(cum)
    p_ex = jnp.exp(cum - ld)
    p_inv = jnp.exp(-cum)
    p_tail = jnp.exp(cum_end - cum)
    kka = kk[sl] * a[sl]
    return dict(rt=r[sl] * p_in, at=-kk[sl] * p_ex, bt=kka * p_inv, kt=k2[sl] * p_inv,
                bh=kka * p_tail, kh=k2[sl] * p_tail, v=v[sl], pc=jnp.exp(cum_end))


def _rwkv_pair_chunk(tb, p, masks):
    strict, incl, eye, lane_lo = masks
    ls = slice(p * LANES, (p + 1) * LANES)

    def stack(x):
        x = x[:, ls]
        return jnp.concatenate([jnp.where(lane_lo, x, 0.0), jnp.where(lane_lo, 0.0, x)], axis=0)

    a_s, r_s, b_s, k_s = stack(tb["at"]), stack(tb["rt"]), stack(tb["bt"]), stack(tb["kt"])
    bh_s, kh_s, v_s = stack(tb["bh"]), stack(tb["kh"]), stack(tb["v"])
    n = 2 * CHUNK
    aa = _dot_nt(jnp.concatenate([a_s, r_s], axis=0), jnp.concatenate([b_s, k_s], axis=0))
    l_ab = jnp.where(strict, aa[:n, :n], 0.0)
    l_ak = jnp.where(strict, aa[:n, n:], 0.0)
    m_rb = jnp.where(incl, aa[n:, :n], 0.0)
    m_rk = jnp.where(incl, aa[n:, n:], 0.0)
    x = jnp.concatenate([a_s, _dot(l_ak, v_s)], axis=1)
    lj = l_ab
    steps = CHUNK.bit_length() - 1
    for j in range(steps):
        if j < steps - 1:
            prod = _dot(lj, jnp.concatenate([lj, x], axis=1))
            lj = prod[:, :n]
            x = x + prod[:, n:]
        else:
            x = x + _dot(lj, x)
    rhs = jnp.concatenate([x, jnp.concatenate([jnp.zeros_like(v_s), v_s], axis=1)], axis=0)
    qo = _dot(jnp.concatenate([m_rb, m_rk], axis=1), rhs)
    gh = _dot_tn(jnp.concatenate([bh_s, kh_s], axis=0), rhs)
    qp = r_s + qo[:, :n]
    op = qo[:, n:]
    g = gh[:, :n] + jnp.where(eye, tb["pc"][:, ls], 0.0)
    h = gh[:, n:]
    return qp, op, g, h


def _rwkv_masks():
    n = 2 * CHUNK
    rr = _iota((n, n), 0)
    cc = _iota((n, n), 1)
    same = (rr >> 6) == (cc >> 6)
    strict = same & ((cc & 63) < (rr & 63))
    incl = same & ((cc & 63) <= (rr & 63))
    eye = rr == cc
    lane_lo = _iota((CHUNK, LANES), 1) < HEAD_DIM
    return strict, incl, eye, lane_lo


def _rwkv_prompt_kernel(z_ref, mu_ref, w0_ref, wup_ref, a0_ref, aup_ref, gup_ref, kk_ref, ka_ref,
                        rk_ref, lng_ref, lnb_ref, wout_ref, y_ref, st_ref, prev_scr, st_scr, *, ct):
    i = pl.program_id(1)

    @pl.when(i == 0)
    def _():
        prev_scr[...] = jnp.zeros_like(prev_scr)
        st_scr[...] = jnp.zeros_like(st_scr)

    zr = z_ref[0]
    row = _iota(zr.shape, 0)
    prevs = jnp.where(row == 0, prev_scr[0:1, :], pltpu.roll(zr, 1, 0))
    prev_scr[0:1, :] = zr[ct - 1:ct, :]
    ones = _head_ones()
    r, k2, v, kk, a, g, logdec = _rwkv_prep(
        zr, prevs, mu_ref[...], w0_ref[...], wup_ref[...], a0_ref[...], aup_ref[...], gup_ref[...],
        kk_ref[...], ka_ref[...], ones)

    rr = _iota((ct, ct), 0)
    cc = _iota((ct, ct), 1)
    tri_bd = jnp.where(((rr >> 6) == (cc >> 6)) & (cc <= rr), 1.0, 0.0).astype(BF16)
    masks = _rwkv_masks()
    n_pairs = R_WIDTH // LANES
    cum_all = _chunk_cumsum(logdec, tri_bd)
    o_rows = []
    for c in range(ct // CHUNK):
        tb = _rwkv_chunk_tables(logdec, cum_all, r, k2, v, kk, a, c)
        o_pairs = []
        for p in range(n_pairs):
            qp, op, gm, hm = _rwkv_pair_chunk(tb, p, masks)
            st = st_scr[p]
            ostk = _dot(qp, st) + op
            st_scr[p] = _dot3(gm, st) + hm
            o_pairs.append(ostk[:CHUNK] + ostk[CHUNK:])
        o_rows.append(jnp.concatenate(o_pairs, axis=1))
    o = o_rows[0] if len(o_rows) == 1 else jnp.concatenate(o_rows, axis=0)
    out = _rwkv_post(o, r, k2, v, g, rk_ref[...], lng_ref[...], lnb_ref[...], ones)
    y_ref[0] = _dot(out, wout_ref[...])
    st_ref[0] = st_scr[...]


def _rwkv_prompt(z_r, prm, *, ct=128):
    B, L, _ = z_r.shape
    assert L % ct == 0 and ct % CHUNK == 0
    n_pairs = R_WIDTH // LANES
    row = lambda x: x.reshape(1, -1)
    full = lambda a: pl.BlockSpec(a.shape, lambda b, i: (0,) * a.ndim)
    params = [row(prm["rwkv_mu"]), row(prm["rwkv_w0"]), prm["rwkv_w_up"], row(prm["rwkv_a0"]),
              prm["rwkv_a_up"], prm["rwkv_g_up"], row(prm["rwkv_k_k"]), row(prm["rwkv_k_a"]),
              row(prm["rwkv_r_k"]), row(prm["rwkv_ln_g"]), row(prm["rwkv_ln_b"]), prm["w_rwkv_out_bf"]]
    y, st = pl.pallas_call(
        functools.partial(_rwkv_prompt_kernel, ct=ct),
        out_shape=(jax.ShapeDtypeStruct((B, L, prm["w_rwkv_out_bf"].shape[1]), F32),
                   jax.ShapeDtypeStruct((B, n_pairs, LANES, LANES), F32)),
        grid=(B, L // ct),
        in_specs=[pl.BlockSpec((1, ct, R_COLS), lambda b, i: (b, i, 0))] + [full(a) for a in params],
        out_specs=(pl.BlockSpec((1, ct, prm["w_rwkv_out_bf"].shape[1]), lambda b, i: (b, i, 0)),
                   pl.BlockSpec((1, n_pairs, LANES, LANES), lambda b, i: (b, 0, 0, 0))),
        scratch_shapes=[pltpu.VMEM((8, R_COLS), F32), pltpu.VMEM((n_pairs, LANES, LANES), F32)],
        compiler_params=_cparams(("parallel", "arbitrary")),
        name="rwkv_prompt",
    )(z_r, *params)
    st = st.reshape(B, n_pairs, 2, HEAD_DIM, 2, HEAD_DIM)
    st = jnp.stack([st[:, :, 0, :, 0, :], st[:, :, 1, :, 1, :]], axis=2)
    return y, jnp.swapaxes(st.reshape(B, R_HEADS, HEAD_DIM, HEAD_DIM), -1, -2)


def _rwkv_sample_kernel(z_ref, sh_ref, s_ref, mu_ref, w0_ref, wup_ref, a0_ref, aup_ref, gup_ref,
                        kk_ref, ka_ref, rk_ref, lng_ref, lnb_ref, o_ref, sn_ref):
    ones = _head_ones()
    zr = z_ref[0]
    r, k2, v, kk, a, g, logdec = _rwkv_prep(
        zr, sh_ref[0], mu_ref[...], w0_ref[...], wup_ref[...], a0_ref[...], aup_ref[...],
        gup_ref[...], kk_ref[...], ka_ref[...], ones)
    w = jnp.exp(logdec)
    b = kk * a
    eye = _iota((HEAD_DIM, HEAD_DIM), 0) == _iota((HEAD_DIM, HEAD_DIM), 1)
    col = lambda x: jnp.sum(jnp.where(eye, x, 0.0), axis=1, keepdims=True)
    o_heads = []
    for h in range(R_HEADS):
        hs = slice(h * HEAD_DIM, (h + 1) * HEAD_DIM)
        s = s_ref[0, h]
        sa = jnp.sum(s * (-kk[:, hs]), axis=1, keepdims=True)
        s_new = s * w[:, hs] + sa * b[:, hs] + col(v[:, hs]) * k2[:, hs]
        sn_ref[0, h] = s_new
        o_col = jnp.sum(s_new * r[:, hs], axis=1, keepdims=True)
        o_heads.append(jnp.sum(jnp.where(eye, o_col, 0.0), axis=0, keepdims=True))
    o = jnp.concatenate(o_heads, axis=1)
    o_ref[0] = _rwkv_post(o, r, k2, v, g, rk_ref[...], lng_ref[...], lnb_ref[...], ones)


def _rwkv_sample(z_r, shift, wkv, prm):
    B = z_r.shape[0]
    row = lambda x: x.reshape(1, -1)
    full = lambda a: pl.BlockSpec(a.shape, lambda b: (0,) * a.ndim)
    params = [row(prm["rwkv_mu"]), row(prm["rwkv_w0"]), prm["rwkv_w_up"], row(prm["rwkv_a0"]),
              prm["rwkv_a_up"], prm["rwkv_g_up"], row(prm["rwkv_k_k"]), row(prm["rwkv_k_a"]),
              row(prm["rwkv_r_k"]), row(prm["rwkv_ln_g"]), row(prm["rwkv_ln_b"])]
    o, s_new = pl.pallas_call(
        _rwkv_sample_kernel,
        out_shape=(jax.ShapeDtypeStruct((B, 1, R_WIDTH), F32),
                   jax.ShapeDtypeStruct(wkv.shape, F32)),
        grid=(B,),
        in_specs=[pl.BlockSpec((1, 1, R_COLS), lambda b: (b, 0, 0)),
                  pl.BlockSpec((1, 1, R_COLS), lambda b: (b, 0, 0)),
                  pl.BlockSpec((1,) + wkv.shape[1:], lambda b: (b, 0, 0, 0))] + [full(a) for a in params],
        out_specs=(pl.BlockSpec((1, 1, R_WIDTH), lambda b: (b, 0, 0)),
                   pl.BlockSpec((1,) + wkv.shape[1:], lambda b: (b, 0, 0, 0))),
        compiler_params=_cparams(("parallel",)),
        name="rwkv_sample",
    )(z_r.reshape(B, 1, R_COLS), shift.reshape(B, 1, R_COLS), wkv, *params)
    return o.reshape(B, R_WIDTH), s_new


def _conv_prompt_kernel(z_ref, dw_ref, dwb_ref, lng_ref, lnb_ref, wout_ref, y_ref, nb_ref, u_scr, *, t):
    i = pl.program_id(1)
    pad = 32

    @pl.when(i == 0)
    def _():
        u_scr[0:pad, :] = jnp.zeros((pad, CONV_CH), F32)

    z = z_ref[0]
    u_scr[pad:pad + t, :] = z[:, :CONV_CH] * _sigmoid(z[:, CONV_CH:])
    acc = jnp.zeros((t, CONV_CH), F32) + dwb_ref[...]
    off = pad - (CONV_K - 1)
    for j in range(CONV_K):
        acc = acc + dw_ref[j:j + 1, :] * u_scr[off + j:off + j + t, :]
    tail = u_scr[t:t + pad, :]
    nb_ref[0] = tail
    u_scr[0:pad, :] = tail
    c = _layer_norm(acc, lng_ref[...], lnb_ref[...])
    c = c * _sigmoid(c)
    y_ref[0] = _dot(c, wout_ref[...])


def _conv_prompt(z_c, prm, *, t=512):
    B, L, _ = z_c.shape
    t = min(t, L)
    assert L % t == 0 and t >= 32
    row = lambda x: x.reshape(1, -1)
    full = lambda a: pl.BlockSpec(a.shape, lambda b, i: (0,) * a.ndim)
    dw = jnp.pad(prm["conv_dw"], ((0, 32 - CONV_K), (0, 0)))
    params = [dw, row(prm["conv_dw_b"]), row(prm["conv_ln_g"]), row(prm["conv_ln_b"]), prm["w_conv_out_bf"]]
    d_out = prm["w_conv_out_bf"].shape[1]
    y, nb = pl.pallas_call(
        functools.partial(_conv_prompt_kernel, t=t),
        out_shape=(jax.ShapeDtypeStruct((B, L, d_out), F32), jax.ShapeDtypeStruct((B, 32, CONV_CH), F32)),
        grid=(B, L // t),
        in_specs=[pl.BlockSpec((1, t, 2 * CONV_CH), lambda b, i: (b, i, 0))] + [full(a) for a in params],
        out_specs=(pl.BlockSpec((1, t, d_out), lambda b, i: (b, i, 0)),
                   pl.BlockSpec((1, 32, CONV_CH), lambda b, i: (b, 0, 0))),
        scratch_shapes=[pltpu.VMEM((32 + t, CONV_CH), F32)],
        compiler_params=_cparams(("parallel", "arbitrary")),
        name="conv_prompt",
    )(z_c, *params)
    return y, nb[:, 32 - (CONV_K - 1):]


def _conv_sample_kernel(z_ref, buf_ref, dw_ref, dwb_ref, lng_ref, lnb_ref, c_ref, u_ref):
    z = z_ref[...]
    u = z[:, :CONV_CH] * _sigmoid(z[:, CONV_CH:])
    u_ref[...] = u
    acc = dwb_ref[...] + dw_ref[CONV_K - 1:CONV_K, :] * u
    for j in range(CONV_K - 1):
        acc = acc + dw_ref[j:j + 1, :] * buf_ref[:, j, :]
    c = _layer_norm(acc, lng_ref[...], lnb_ref[...])
    c_ref[...] = c * _sigmoid(c)


def _conv_sample(z_c, buf, prm):
    B = z_c.shape[0]
    row = lambda x: x.reshape(1, -1)
    args = [z_c, buf, prm["conv_dw"], row(prm["conv_dw_b"]), row(prm["conv_ln_g"]), row(prm["conv_ln_b"])]
    return pl.pallas_call(
        _conv_sample_kernel,
        out_shape=(jax.ShapeDtypeStruct((B, CONV_CH), F32), jax.ShapeDtypeStruct((B, CONV_CH), F32)),
        compiler_params=pltpu.CompilerParams(vmem_limit_bytes=VMEM_LIMIT),
        name="conv_sample",
    )(*args)


def _attn_group(q_ref, k_ref, v_ref, o_ref, m_scr, l_scr, acc_scr, *, seq, dil, nk, first, last):
    nb = (seq // dil) // Q_BLOCK
    qi = _iota((Q_BLOCK, 2 * Q_BLOCK), 0)
    kj = _iota((Q_BLOCK, 2 * Q_BLOCK), 1)
    dist = qi + Q_BLOCK - kj
    in_win = (dist >= 0) & (dist <= nk)
    is_cur = kj >= Q_BLOCK
    lane_lo = _iota((Q_BLOCK, LANES), 1) < HEAD_DIM

    def rows(start):
        return pl.ds(start, Q_BLOCK) if dil == 1 else pl.ds(start, Q_BLOCK, stride=dil)

    def body(it, carry):
        d = it // nb
        n = it - d * nb
        q0 = d + n * (Q_BLOCK * dil)
        p0 = jnp.maximum(q0 - Q_BLOCK * dil, d)
        qb = q_ref[0, rows(q0), :] * ATTN_SCALE
        kb = jnp.concatenate([k_ref[0, rows(p0), :], k_ref[0, rows(q0), :]], axis=0).astype(BF16)
        vb = jnp.concatenate([v_ref[0, rows(p0), :], v_ref[0, rows(q0), :]], axis=0).astype(BF16)
        mask = in_win & (is_cur | (n > 0))
        ms, ls, os = [], [], []
        for lo in (True, False):
            qh = jnp.where(lane_lo == lo, qb, 0.0)
            s = jnp.where(mask, _dot_nt(qh, kb), NEG_BIG)
            m = jnp.max(s, axis=1, keepdims=True)
            p = jnp.exp(s - m)
            ms.append(m)
            ls.append(jnp.sum(p, axis=1, keepdims=True))
            os.append(jnp.dot(p.astype(BF16), vb, preferred_element_type=F32))
        m_b = jnp.where(lane_lo, ms[0], ms[1])
        l_b = jnp.where(lane_lo, ls[0], ls[1])
        o_b = jnp.where(lane_lo, os[0], os[1])
        r = rows(q0)
        if not first:
            m_old = m_scr[r, :]
            m_new = jnp.maximum(m_old, m_b)
            a_old = jnp.exp(m_old - m_new)
            a_new = jnp.exp(m_b - m_new)
            l_b = l_scr[r, :] * a_old + l_b * a_new
            o_b = acc_scr[r, :] * a_old + o_b * a_new
            m_b = m_new
        if last:
            o_ref[0, r, :] = o_b / l_b
        else:
            m_scr[r, :] = m_b
            l_scr[r, :] = l_b
            acc_scr[r, :] = o_b
        return carry

    lax.fori_loop(0, seq // Q_BLOCK, body, 0)


def _attn_prompt_kernel(q_ref, k_ref, v_ref, o_ref, m_scr, l_scr, acc_scr, *, seq):
    g = pl.program_id(2)
    for gi, (win, dil) in enumerate(SWA_GROUPS):
        @pl.when(g == gi)
        def _(dil=dil, nk=win // dil, gi=gi):
            _attn_group(q_ref, k_ref, v_ref, o_ref, m_scr, l_scr, acc_scr, seq=seq, dil=dil, nk=nk,
                        first=gi == 0, last=gi == N_GROUPS - 1)


def _attn_prompt(z_qkv):
    B, S, _ = z_qkv.shape
    for win, dil in SWA_GROUPS:
        assert S % (dil * Q_BLOCK) == 0 and win // dil <= Q_BLOCK
    pairs = G_WIDTH // LANES
    blk = lambda base: pl.BlockSpec((1, S, LANES), lambda b, p, g: (b, 0, base + g * pairs + p))
    return pl.pallas_call(
        functools.partial(_attn_prompt_kernel, seq=S),
        out_shape=jax.ShapeDtypeStruct((B, S, G_WIDTH), F32),
        grid=(B, pairs, N_GROUPS),
        in_specs=[blk(0), blk(A_WIDTH // LANES), blk(2 * A_WIDTH // LANES)],
        out_specs=pl.BlockSpec((1, S, LANES), lambda b, p, g: (b, 0, p)),
        scratch_shapes=[pltpu.VMEM((S, LANES), F32)] * 3,
        compiler_params=_cparams(("parallel", "parallel", "arbitrary")),
        name="attn_prompt",
    )(z_qkv, z_qkv, z_qkv)


def _attn_sample_kernel(q_ref, kn_ref, vn_ref, ca_ref, cb_ref, cc_ref, o_ref, *, bs):
    caches = (ca_ref, cb_ref, cc_ref)
    hrow = _iota((8, G_WIDTH), 0)
    hmask = hrow == (_iota((8, G_WIDTH), 1) >> 6)
    outs = []
    for n in range(bs):
        m_run = l_run = acc = None
        for gi in range(N_GROUPS):
            gs = slice(gi * G_WIDTH, (gi + 1) * G_WIDTH)
            q = q_ref[n:n + 1, gs] * ATTN_SCALE
            qr = jnp.where(hmask, q, 0.0).astype(BF16)
            kc = caches[gi][n, :, 0:G_WIDTH].astype(BF16)
            vc = caches[gi][n, :, G_WIDTH:2 * G_WIDTH].astype(BF16)
            kn = kn_ref[n:n + 1, gs].astype(BF16).astype(F32)
            vn = vn_ref[n:n + 1, gs].astype(BF16).astype(F32)
            s = _dot_nt(qr, kc)
            s_n = jnp.sum(qr.astype(F32) * kn, axis=1, keepdims=True)
            m = jnp.maximum(jnp.max(s, axis=1, keepdims=True), s_n)
            p = jnp.exp(s - m)
            p_n = jnp.exp(s_n - m)
            l = jnp.sum(p, axis=1, keepdims=True) + p_n
            o = jnp.dot(p.astype(BF16), vc, preferred_element_type=F32) + p_n.astype(BF16).astype(F32) * vn
            if m_run is None:
                m_run, l_run, acc = m, l, o
            else:
                m_new = jnp.maximum(m_run, m)
                a_old = jnp.exp(m_run - m_new)
                a_new = jnp.exp(m - m_new)
                l_run = l_run * a_old + l * a_new
                acc = acc * a_old + o * a_new
                m_run = m_new
        outs.append(jnp.sum(jnp.where(hmask, acc / l_run, 0.0), axis=0, keepdims=True))
    o_ref[...] = jnp.concatenate(outs, axis=0)


def _attn_sample(q, k_new, v_new, caches, *, bs=8):
    B = q.shape[0]
    strided = []
    for (win, dil), c in zip(SWA_GROUPS, caches):
        n_buf = c.shape[1]
        assert n_buf == win and n_buf % dil == 0
        strided.append(c.reshape(B, n_buf // dil, dil * 2 * G_WIDTH))
    rows = [c.shape[1] for c in strided]
    vec = pl.BlockSpec((bs, A_WIDTH), lambda i: (i, 0))
    return pl.pallas_call(
        functools.partial(_attn_sample_kernel, bs=bs),
        out_shape=jax.ShapeDtypeStruct((B, G_WIDTH), F32),
        grid=(B // bs,),
        in_specs=[vec, vec, vec] + [pl.BlockSpec((bs, r, 2 * G_WIDTH), lambda i: (i, 0, 0)) for r in rows],
        out_specs=pl.BlockSpec((bs, G_WIDTH), lambda i: (i, 0)),
        compiler_params=_cparams(("parallel",)),
        name="attn_sample",
    )(q, k_new, v_new, *strided)


def _merge_kernel(zg_ref, yr_ref, yc_ref, ao_ref, x_ref, gb_ref, wa_ref, wo_ref, g_ref, b_ref, h_ref, *,
                  alpha):
    d = x_ref.shape[1]
    zg = zg_ref[...]
    gb = gb_ref[...]
    gate = lambda i: _sigmoid(zg[:, i * d:(i + 1) * d] + gb[:, i * d:(i + 1) * d])
    y_a = _dot(ao_ref[...], wa_ref[...])
    merged = gate(0) * yr_ref[...] + gate(1) * yc_ref[...] + gate(2) * y_a
    h_ref[...] = _layer_norm(alpha * x_ref[...] + _dot(merged, wo_ref[...]), g_ref[...], b_ref[...])


def _merge(z_g, y_r, y_c, a_o, x, prm, *, alpha, t=512):
    M, D = x.shape
    t = min(t, M)
    assert M % t == 0
    row = lambda a: a.reshape(1, -1)
    tile = lambda a: pl.BlockSpec((t, a.shape[1]), lambda i: (i, 0))
    full = lambda a: pl.BlockSpec(a.shape, lambda i: (0, 0))
    acts = [z_g, y_r, y_c, a_o, x]
    params = [row(prm["gate_b"]), prm["w_attn_out_bf"], prm["w_o_bf"], row(prm["ln1_g"]), row(prm["ln1_b"])]
    return pl.pallas_call(
        functools.partial(_merge_kernel, alpha=alpha),
        out_shape=jax.ShapeDtypeStruct((M, D), F32),
        grid=(M // t,),
        in_specs=[tile(a) for a in acts] + [full(a) for a in params],
        out_specs=pl.BlockSpec((t, D), lambda i: (i, 0)),
        compiler_params=_cparams(("parallel",)),
        name="merge",
    )(*acts, *params)


def _ffn_kernel(*refs, alpha, t, tiles_per_seq, sample):
    if sample:
        (h_ref, pg_ref, pu_ref, wg_ref, wu_ref, dg_ref, du_ref, bg_ref, bu_ref, wo_ref, g_ref, b_ref,
         o_ref, ng_ref, nu_ref, acc_scr) = refs
    else:
        (h_ref, wg_ref, wu_ref, dg_ref, du_ref, bg_ref, bu_ref, wo_ref, g_ref, b_ref,
         o_ref, ng_ref, nu_ref, acc_scr, cg_scr, cu_scr) = refs
    i = pl.program_id(0)
    j = pl.program_id(1)
    nj = pl.num_programs(1)
    hb = h_ref[...].astype(BF16)

    def conv(u, dw_ref, bias_ref, p1, p2):
        return dw_ref[0:1, :] * p2 + dw_ref[1:2, :] * p1 + dw_ref[2:3, :] * u + bias_ref[...]

    def branch(w_ref, dw_ref, bias_ref, new_ref, prev_ref_or_scr):
        u = jnp.dot(hb, w_ref[...], preferred_element_type=F32)
        if sample:
            new_ref[...] = u
            p2, p1 = prev_ref_or_scr[0], prev_ref_or_scr[1]
        else:
            new_ref[0] = u[t - 2:t, :]
            row = _iota(u.shape, 0)
            fresh = (i % tiles_per_seq) == 0
            c1 = jnp.where(fresh, 0.0, prev_ref_or_scr[j, 7:8, :])
            c2 = jnp.where(fresh, 0.0, prev_ref_or_scr[j, 6:7, :])
            p1 = jnp.where(row == 0, c1, pltpu.roll(u, 1, 0))
            p2 = jnp.where(row == 0, c2, jnp.where(row == 1, c1, pltpu.roll(u, 2, 0)))
            prev_ref_or_scr[j, :, :] = u[t - 8:t, :]
        return conv(u, dw_ref, bias_ref, p1, p2)

    if sample:
        y_g = branch(wg_ref, dg_ref, bg_ref, ng_ref, pg_ref)
        y_u = branch(wu_ref, du_ref, bu_ref, nu_ref, pu_ref)
    else:
        y_g = branch(wg_ref, dg_ref, bg_ref, ng_ref, cg_scr)
        y_u = branch(wu_ref, du_ref, bu_ref, nu_ref, cu_scr)
    act = (y_g * _sigmoid(y_g)) * y_u
    part = _dot(act, wo_ref[...])

    @pl.when(j == 0)
    def _():
        acc_scr[...] = part

    @pl.when(j > 0)
    def _():
        acc_scr[...] += part

    @pl.when(j == nj - 1)
    def _():
        o_ref[...] = _layer_norm(alpha * h_ref[...] + acc_scr[...], g_ref[...], b_ref[...])


def _ffn(h, prm, *, alpha, seq_len, prev=None, t=512, tn=1408):
    M, D = h.shape
    d_ff = prm["w_ffn_out_bf"].shape[0]
    sample = prev is not None
    t = M if sample else min(t, seq_len)
    tn = min(tn, d_ff)
    assert M % t == 0 and d_ff % tn == 0 and (sample or (seq_len % t == 0 and t >= 8))
    J = d_ff // tn
    row = lambda a: a.reshape(1, -1)
    w_in, dw, dwb = prm["w_ffn_in_bf"], prm["ffn_dw"], row(prm["ffn_dw_b"])
    col_g = lambda shape: pl.BlockSpec(shape, lambda i, j: (0, j))
    col_u = lambda shape: pl.BlockSpec(shape, lambda i, j: (0, J + j))
    const = lambda a: pl.BlockSpec(a.shape, lambda i, j: (0, 0))
    ins, specs = [h], [pl.BlockSpec((t, D), lambda i, j: (i, 0))]
    if sample:
        pstack = jnp.stack(prev)
        ins += [pstack, pstack]
        specs += [pl.BlockSpec((2, M, tn), lambda i, j: (0, 0, j)), pl.BlockSpec((2, M, tn), lambda i, j: (0, 0, J + j))]
    ins += [w_in, w_in, dw, dw, dwb, dwb, prm["w_ffn_out_bf"], row(prm["ln2_g"]), row(prm["ln2_b"])]
    specs += [col_g((D, tn)), col_u((D, tn)), col_g((FFN_K, tn)), col_u((FFN_K, tn)), col_g((1, tn)),
              col_u((1, tn)), pl.BlockSpec((tn, D), lambda i, j: (j, 0)), const(row(prm["ln2_g"])),
              const(row(prm["ln2_b"]))]
    if sample:
        new_shape = jax.ShapeDtypeStruct((M, d_ff), F32)
        new_spec = pl.BlockSpec((M, tn), lambda i, j: (0, j))
        scratch = [pltpu.VMEM((t, D), F32)]
        tiles_per_seq = 1
    else:
        tiles_per_seq = seq_len // t
        new_shape = jax.ShapeDtypeStruct((M // t, 2, d_ff), F32)
        new_spec = pl.BlockSpec((1, 2, tn), lambda i, j: (i, 0, j))
        scratch = [pltpu.VMEM((t, D), F32), pltpu.VMEM((J, 8, tn), F32), pltpu.VMEM((J, 8, tn), F32)]
    out, new_g, new_u = pl.pallas_call(
        functools.partial(_ffn_kernel, alpha=alpha, t=t, tiles_per_seq=tiles_per_seq, sample=sample),
        out_shape=(jax.ShapeDtypeStruct((M, D), F32), new_shape, new_shape),
        grid=(M // t, J),
        in_specs=specs,
        out_specs=(pl.BlockSpec((t, D), lambda i, j: (i, 0)), new_spec, new_spec),
        scratch_shapes=scratch,
        compiler_params=_cparams(("arbitrary", "arbitrary")),
        name="ffn_sample" if sample else "ffn_prompt",
    )(*ins)
    if not sample:
        new_g = new_g[tiles_per_seq - 1::tiles_per_seq]
        new_u = new_u[tiles_per_seq - 1::tiles_per_seq]
    return out, new_g, new_u


def _layer_weights(l, w_in, w_rwkv_out, w_conv_out, w_attn_out, w_o, w_ffn_in, w_ffn_out, small):
    prm = {k: v[l] for k, v in small.items()}
    wi = w_in[l]
    o_c = R_COLS
    o_q = o_c + 2 * CONV_CH
    o_g = o_q + 3 * A_WIDTH
    prm["w_in_r"] = wi[:, :o_c].astype(BF16)
    prm["w_in_c"] = wi[:, o_c:o_q].astype(BF16)
    prm["w_in_qkv"] = wi[:, o_q:o_g].astype(BF16)
    prm["w_in_g"] = wi[:, o_g:].astype(BF16)
    prm["w_rwkv_out_bf"] = w_rwkv_out[l].astype(BF16)
    prm["w_conv_out_bf"] = w_conv_out[l].astype(BF16)
    prm["w_attn_out_bf"] = w_attn_out[l].astype(BF16)
    prm["w_o_bf"] = w_o[l].astype(BF16)
    prm["w_ffn_in_bf"] = w_ffn_in[l].astype(BF16)
    prm["w_ffn_out_bf"] = w_ffn_out[l].astype(BF16)
    return prm


def _project(x2, prm, tm):
    z_r = _mm(x2, prm["w_in_r"], tm=tm, tn=R_COLS)
    z_c = _mm(x2, prm["w_in_c"], tm=tm, tn=2 * CONV_CH)
    z_qkv = _mm(x2, prm["w_in_qkv"], tm=tm, tn=3 * A_WIDTH // 2)
    z_g = _mm(x2, prm["w_in_g"], tm=tm, tn=1024)
    return z_r, z_c, z_qkv, z_g


def _kv_rows(z_qkv, gi):
    k = z_qkv[..., A_WIDTH + gi * G_WIDTH:A_WIDTH + (gi + 1) * G_WIDTH]
    v = z_qkv[..., 2 * A_WIDTH + gi * G_WIDTH:2 * A_WIDTH + (gi + 1) * G_WIDTH]
    kv = jnp.stack([k, v], axis=-2)
    return kv.reshape(kv.shape[:-1] + (G_HEADS, HEAD_DIM))


def _prompt_layer(x, prm, alpha):
    B, L, D = x.shape
    x2 = x.reshape(B * L, D)
    z_r, z_c, z_qkv, z_g = _project(x2, prm, tm=1024)
    z_r = z_r.reshape(B, L, -1)
    z_qkv = z_qkv.reshape(B, L, -1)
    y_r, new_wkv = _rwkv_prompt(z_r, prm)
    y_c, new_conv = _conv_prompt(z_c.reshape(B, L, -1), prm)
    a_o = _attn_prompt(z_qkv)
    h = _merge(z_g, y_r.reshape(B * L, D), y_c.reshape(B * L, D), a_o.reshape(B * L, -1), x2, prm, alpha=alpha)
    out, nf_g, nf_u = _ffn(h, prm, alpha=alpha, seq_len=L)
    new_kv = [_kv_rows(z_qkv[:, L - min(win, L):], gi) for gi, (win, _) in enumerate(SWA_GROUPS)]
    return (out.reshape(B, L, D), z_r[:, L - 1:], new_wkv, new_conv, new_kv,
            jnp.concatenate([nf_g, nf_u], axis=-1))


def _sample_layer(x, prm, alpha, shift, wkv, conv_buf, ffn_buf, kv_bufs):
    B, L, D = x.shape
    assert L == 1
    x2 = x.reshape(B, D)
    z_r, z_c, z_qkv, z_g = _project(x2, prm, tm=B)
    o_r, new_wkv = _rwkv_sample(z_r, shift.reshape(B, R_COLS), wkv, prm)
    y_r = _mm(o_r, prm["w_rwkv_out_bf"], tm=B, tn=1024)
    c, u = _conv_sample(z_c, conv_buf, prm)
    y_c = _mm(c, prm["w_conv_out_bf"], tm=B, tn=1024)
    caches = [b.reshape(B, b.shape[1], 2 * G_WIDTH) for b in kv_bufs]
    a_o = _attn_sample(z_qkv[:, :A_WIDTH], z_qkv[:, A_WIDTH:2 * A_WIDTH], z_qkv[:, 2 * A_WIDTH:], caches)
    h = _merge(z_g, y_r, y_c, a_o, x2, prm, alpha=alpha)
    out, u_g, u_u = _ffn(h, prm, alpha=alpha, seq_len=1, prev=(ffn_buf[:, 0], ffn_buf[:, 1]))
    new_kv = []
    for gi, ((win, _), buf) in enumerate(zip(SWA_GROUPS, kv_bufs)):
        keep = min(win, buf.shape[1] + 1)
        full = jnp.concatenate([buf, _kv_rows(z_qkv, gi)[:, None]], axis=1)
        new_kv.append(full[:, full.shape[1] - keep:])
    new_conv = jnp.concatenate([conv_buf[:, 1:], u[:, None]], axis=1)
    new_ffn = jnp.concatenate([ffn_buf[:, 1:], jnp.concatenate([u_g, u_u], axis=-1)[:, None]], axis=1)
    return out.reshape(B, 1, D), z_r.reshape(B, 1, R_COLS), new_wkv, new_conv, new_kv, new_ffn


def kernel(x_prompt, x_sample, state_shift, state_wkv, state_conv, cache_swa_a, cache_swa_b, cache_swa_c, state_ffn, w_in, rwkv_mu, rwkv_w0, rwkv_w_up, rwkv_a0, rwkv_a_up, rwkv_g_up, rwkv_k_k, rwkv_k_a, rwkv_r_k, rwkv_ln_g, rwkv_ln_b, w_rwkv_out, conv_dw, conv_dw_b, conv_ln_g, conv_ln_b, w_conv_out, w_attn_out, gate_b, w_o, ln1_g, ln1_b, w_ffn_in, ffn_dw, ffn_dw_b, w_ffn_out, ln2_g, ln2_b):
    depth = w_in.shape[0]
    alpha = (2 * depth) ** 0.25
    small = dict(rwkv_mu=rwkv_mu, rwkv_w0=rwkv_w0, rwkv_w_up=rwkv_w_up, rwkv_a0=rwkv_a0, rwkv_a_up=rwkv_a_up,
                 rwkv_g_up=rwkv_g_up, rwkv_k_k=rwkv_k_k, rwkv_k_a=rwkv_k_a,
                 rwkv_r_k=rwkv_r_k.reshape(depth, -1), rwkv_ln_g=rwkv_ln_g, rwkv_ln_b=rwkv_ln_b,
                 conv_dw=conv_dw, conv_dw_b=conv_dw_b, conv_ln_g=conv_ln_g, conv_ln_b=conv_ln_b,
                 gate_b=gate_b.reshape(depth, -1), ln1_g=ln1_g, ln1_b=ln1_b, ffn_dw=ffn_dw, ffn_dw_b=ffn_dw_b,
                 ln2_g=ln2_g, ln2_b=ln2_b)
    hp, hs = x_prompt, x_sample
    outs_p, outs_s = [], []
    for l in range(depth):
        prm = _layer_weights(l, w_in, w_rwkv_out, w_conv_out, w_attn_out, w_o, w_ffn_in, w_ffn_out, small)
        res_p = _prompt_layer(hp, prm, alpha)
        hp = res_p[0]
        outs_p.append(res_p[1:])
        res_s = _sample_layer(hs, prm, alpha, state_shift[l], state_wkv[l], state_conv[l], state_ffn[l],
                              (cache_swa_a[l], cache_swa_b[l], cache_swa_c[l]))
        hs = res_s[0]
        outs_s.append(res_s[1:])
    stk = lambda outs, f: jnp.stack([f(o) for o in outs])
    res = [hp, hs]
    for f in (lambda o: o[0], lambda o: o[1], lambda o: o[2], lambda o: o[3][0], lambda o: o[3][1],
              lambda o: o[3][2], lambda o: o[4]):
        res += [stk(outs_p, f), stk(outs_s, f)]
    return tuple(res)
```

```python
import functools

import jax
import jax.numpy as jnp
from jax import lax
from jax.experimental import pallas as pl
from jax.experimental.pallas import tpu as pltpu

F32 = jnp.float32
BF16 = jnp.bfloat16

HEAD_DIM = 64
R_HEADS = 8
R_WIDTH = R_HEADS * HEAD_DIM
DECAY_LORA = 64
ICLR_LORA = 64
GATE_LORA = 128
R_COLS = 3 * R_WIDTH + DECAY_LORA + ICLR_LORA + GATE_LORA
LNX_EPS = 64e-5
CONV_CH = 512
CONV_K = 31
SWA_GROUPS = ((128, 1), (512, 4), (2048, 16))
N_GROUPS = len(SWA_GROUPS)
G_HEADS = 4
G_WIDTH = G_HEADS * HEAD_DIM
A_WIDTH = N_GROUPS * G_WIDTH
Q_BLOCK = 128
ATTN_SCALE = HEAD_DIM ** -0.5
N_BRANCH = 3
FFN_K = 3
LN_EPS = 1e-5
NEG_BIG = -1e30

LANES = 128
CHUNK = 64
SUBLANES = 8
CONV_ROWS = 32
SEG_BLOCK = 256
ATTN_UNROLL = 4
VMEM_LIMIT = 56 * 1024 * 1024


def _cparams(sem):
    return pltpu.CompilerParams(dimension_semantics=sem, vmem_limit_bytes=VMEM_LIMIT)


def _dot(a, b):
    return jnp.dot(a.astype(BF16), b.astype(BF16), preferred_element_type=F32)


def _dot_nt(a, b):
    return lax.dot_general(a.astype(BF16), b.astype(BF16), (((1,), (1,)), ((), ())),
                           preferred_element_type=F32)


def _dot_tn(a, b):
    return lax.dot_general(a.astype(BF16), b.astype(BF16), (((0,), (0,)), ((), ())),
                           preferred_element_type=F32)


def _split2(x):
    hi = x.astype(BF16)
    lo = (x - hi.astype(F32)).astype(BF16)
    return hi, lo


def _split3(x):
    h1 = x.astype(BF16)
    r1 = x - h1.astype(F32)
    h2 = r1.astype(BF16)
    h3 = (r1 - h2.astype(F32)).astype(BF16)
    return h1, h2, h3


def _dot3(a, b):
    ah, al = _split2(a)
    bh, bl = _split2(b)
    d = lambda x, y: jnp.dot(x, y, preferred_element_type=F32)
    return d(ah, bh) + (d(ah, bl) + d(al, bh))


def _sigmoid(x):
    return 1.0 / (1.0 + jnp.exp(-x))


def _layer_norm(x, g, b):
    mu = jnp.mean(x, axis=-1, keepdims=True)
    xc = x - mu
    var = jnp.mean(xc * xc, axis=-1, keepdims=True)
    return xc * lax.rsqrt(var + LN_EPS) * g + b


def _iota(shape, dim):
    return lax.broadcasted_iota(jnp.int32, shape, dim)


def _head_ones():
    r = _iota((SEG_BLOCK, SEG_BLOCK), 0)
    c = _iota((SEG_BLOCK, SEG_BLOCK), 1)
    return jnp.where((r >> 6) == (c >> 6), 1.0, 0.0).astype(BF16)


def _segsum(x, ones):
    outs = [_dot(x[:, p:p + SEG_BLOCK], ones) for p in range(0, x.shape[1], SEG_BLOCK)]
    return outs[0] if len(outs) == 1 else jnp.concatenate(outs, axis=1)


def _mm_kernel(x_ref, w_ref, o_ref):
    o_ref[...] = jnp.dot(x_ref[...].astype(BF16), w_ref[...],
                         preferred_element_type=F32).astype(o_ref.dtype)


def _mm(x, w, *, tm, tn, out_dtype=F32):
    M, K = x.shape
    N = w.shape[1]
    tm = min(tm, M)
    tn = min(tn, N)
    assert M % tm == 0 and N % tn == 0
    return pl.pallas_call(
        _mm_kernel,
        out_shape=jax.ShapeDtypeStruct((M, N), out_dtype),
        grid=(M // tm, N // tn),
        in_specs=[pl.BlockSpec((tm, K), lambda i, j: (i, 0)),
                  pl.BlockSpec((K, tn), lambda i, j: (0, j))],
        out_specs=pl.BlockSpec((tm, tn), lambda i, j: (i, j)),
        compiler_params=_cparams(("parallel", "parallel")),
        name="mm",
    )(x, w)


def _rwkv_prep(zr, prevs, mu, w0, w_up, a0, a_up, g_up, k_k, k_a, ones):
    zs = zr + (prevs - zr) * mu
    r = zs[:, 0:R_WIDTH]
    k = zs[:, R_WIDTH:2 * R_WIDTH]
    v = zs[:, 2 * R_WIDTH:3 * R_WIDTH]
    o1 = 3 * R_WIDTH
    wd = zs[:, o1:o1 + DECAY_LORA]
    ad = zs[:, o1 + DECAY_LORA:o1 + DECAY_LORA + ICLR_LORA]
    gd = zs[:, o1 + DECAY_LORA + ICLR_LORA:R_COLS]
    nx = -(w0 + _dot(jnp.tanh(wd), w_up))
    softplus = jnp.maximum(nx, 0.0) + jnp.log(1.0 + jnp.exp(-jnp.abs(nx)))
    logdec = -jnp.exp(-softplus - 0.5)
    a = _sigmoid(a0 + _dot(ad, a_up))
    g = _dot(_sigmoid(gd), g_up)
    kk = k * k_k
    kk = kk / jnp.maximum(jnp.sqrt(_segsum(kk * kk, ones)), 1e-12)
    k2 = k * (1.0 + (a - 1.0) * k_a)
    return r, k2, v, kk, a, g, logdec


def _rwkv_post(o, r, k2, v, g, r_k, ln_g, ln_b, ones):
    inv = 1.0 / HEAD_DIM
    m = _segsum(o, ones) * inv
    oc = o - m
    var = _segsum(oc * oc, ones) * inv
    on = oc * lax.rsqrt(var + LNX_EPS) * ln_g + ln_b
    bonus = _segsum(r * k2 * r_k, ones) * v
    return (on + bonus) * g


def _chunk_cumsum(logdec, tri_bd):
    h1, h2, h3 = _split3(logdec)
    d = lambda x: jnp.dot(tri_bd, x, preferred_element_type=F32)
    return d(h1) + d(h2) + d(h3)


def _rwkv_chunk_tables(logdec, cum_all, r, k2, v, kk, a, c):
    sl = slice(c * CHUNK, (c + 1) * CHUNK)
    cum = cum_all[sl]
    ld = logdec[sl]
    cum_end = cum[CHUNK - 1:CHUNK, :]
    p_in = jnp.exp(cum)
    p_ex = jnp.exp(cum - ld)
    p_inv = jnp.exp(-cum)
    p_tail = jnp.exp(cum_end - cum)
    kka = kk[sl] * a[sl]
    return dict(rt=r[sl] * p_in, at=-kk[sl] * p_ex, bt=kka * p_inv, kt=k2[sl] * p_inv,
                bh=kka * p_tail, kh=k2[sl] * p_tail, v=v[sl], pc=jnp.exp(cum_end))


def _rwkv_units(tables, n_pairs, masks):
    strict, incl, eye, lane_lo = masks
    n = 2 * CHUNK
    units = [(tb, slice(p * LANES, (p + 1) * LANES)) for tb in tables for p in range(n_pairs)]

    def stack(x, ls):
        x = x[:, ls]
        return jnp.concatenate([jnp.where(lane_lo, x, 0.0), jnp.where(lane_lo, 0.0, x)], axis=0)

    ops = [{k: stack(tb[k], ls) for k in ("at", "rt", "bt", "kt", "bh", "kh", "v")} for tb, ls in units]
    aa = [_dot_nt(jnp.concatenate([o["at"], o["rt"]], axis=0), jnp.concatenate([o["bt"], o["kt"]], axis=0))
          for o in ops]
    lj = [jnp.where(strict, a[:n, :n], 0.0) for a in aa]
    x = [jnp.concatenate([o["at"], _dot(jnp.where(strict, a[:n, n:], 0.0), o["v"])], axis=1)
         for o, a in zip(ops, aa)]
    steps = CHUNK.bit_length() - 1
    for j in range(steps):
        if j < steps - 1:
            prod = [_dot(l, jnp.concatenate([l, xx], axis=1)) for l, xx in zip(lj, x)]
            lj = [pr[:, :n] for pr in prod]
            x = [xx + pr[:, n:] for xx, pr in zip(x, prod)]
        else:
            x = [xx + _dot(l, xx) for l, xx in zip(lj, x)]
    rhs = [jnp.concatenate([xx, jnp.concatenate([jnp.zeros_like(o["v"]), o["v"]], axis=1)], axis=0)
           for xx, o in zip(x, ops)]
    qo = [_dot(jnp.concatenate([jnp.where(incl, a[n:, :n], 0.0), jnp.where(incl, a[n:, n:], 0.0)], axis=1), rh)
          for a, rh in zip(aa, rhs)]
    gh = [_dot_tn(jnp.concatenate([o["bh"], o["kh"]], axis=0), rh) for o, rh in zip(ops, rhs)]
    out = []
    for (tb, ls), o, q, g in zip(units, ops, qo, gh):
        out.append((o["rt"] + q[:, :n], q[:, n:], g[:, :n] + jnp.where(eye, tb["pc"][:, ls], 0.0), g[:, n:]))
    return out


def _rwkv_masks():
    n = 2 * CHUNK
    rr = _iota((n, n), 0)
    cc = _iota((n, n), 1)
    same = (rr >> 6) == (cc >> 6)
    strict = same & ((cc & 63) < (rr & 63))
    incl = same & ((cc & 63) <= (rr & 63))
    eye = rr == cc
    lane_lo = _iota((CHUNK, LANES), 1) < HEAD_DIM
    return strict, incl, eye, lane_lo


def _rwkv_prompt_kernel(z_ref, mu_ref, w0_ref, wup_ref, a0_ref, aup_ref, gup_ref, kk_ref, ka_ref,
                        rk_ref, lng_ref, lnb_ref, wout_ref, y_ref, st_ref, prev_scr, st_scr, *, ct):
    i = pl.program_id(1)

    @pl.when(i == 0)
    def _():
        prev_scr[...] = jnp.zeros_like(prev_scr)
        st_scr[...] = jnp.zeros_like(st_scr)

    zr = z_ref[0]
    row = _iota(zr.shape, 0)
    prevs = jnp.where(row == 0, prev_scr[0:1, :], pltpu.roll(zr, 1, 0))
    prev_scr[0:1, :] = zr[ct - 1:ct, :]
    ones = _head_ones()
    r, k2, v, kk, a, g, logdec = _rwkv_prep(
        zr, prevs, mu_ref[...], w0_ref[...], wup_ref[...], a0_ref[...], aup_ref[...], gup_ref[...],
        kk_ref[...], ka_ref[...], ones)

    rr = _iota((ct, ct), 0)
    cc = _iota((ct, ct), 1)
    tri_bd = jnp.where(((rr >> 6) == (cc >> 6)) & (cc <= rr), 1.0, 0.0).astype(BF16)
    masks = _rwkv_masks()
    n_pairs = R_WIDTH // LANES
    cum_all = _chunk_cumsum(logdec, tri_bd)
    n_chunks = ct // CHUNK
    tables = [_rwkv_chunk_tables(logdec, cum_all, r, k2, v, kk, a, c) for c in range(n_chunks)]
    units = _rwkv_units(tables, n_pairs, masks)
    states = [st_scr[p] for p in range(n_pairs)]
    o_rows = []
    for c in range(n_chunks):
        cur = units[c * n_pairs:(c + 1) * n_pairs]
        ostk = [_dot(qp, st) + op for (qp, op, _, _), st in zip(cur, states)]
        states = [_dot3(gm, st) + hm for (_, _, gm, hm), st in zip(cur, states)]
        o_rows.append(jnp.concatenate([o[:CHUNK] + o[CHUNK:] for o in ostk], axis=1))
    for p in range(n_pairs):
        st_scr[p] = states[p]
    o = o_rows[0] if len(o_rows) == 1 else jnp.concatenate(o_rows, axis=0)
    out = _rwkv_post(o, r, k2, v, g, rk_ref[...], lng_ref[...], lnb_ref[...], ones)
    y_ref[0] = _dot(out, wout_ref[...])
    st_ref[0] = st_scr[...]


def _rwkv_prompt(z_r, prm, *, ct=256):
    B, L, _ = z_r.shape
    assert L % ct == 0 and ct % CHUNK == 0
    n_pairs = R_WIDTH // LANES
    row = lambda x: x.reshape(1, -1)
    full = lambda a: pl.BlockSpec(a.shape, lambda b, i: (0,) * a.ndim)
    params = [row(prm["rwkv_mu"]), row(prm["rwkv_w0"]), prm["rwkv_w_up"], row(prm["rwkv_a0"]),
              prm["rwkv_a_up"], prm["rwkv_g_up"], row(prm["rwkv_k_k"]), row(prm["rwkv_k_a"]),
              row(prm["rwkv_r_k"]), row(prm["rwkv_ln_g"]), row(prm["rwkv_ln_b"]), prm["w_rwkv_out_bf"]]
    y, st = pl.pallas_call(
        functools.partial(_rwkv_prompt_kernel, ct=ct),
        out_shape=(jax.ShapeDtypeStruct((B, L, prm["w_rwkv_out_bf"].shape[1]), F32),
                   jax.ShapeDtypeStruct((B, n_pairs, LANES, LANES), F32)),
        grid=(B, L // ct),
        in_specs=[pl.BlockSpec((1, ct, R_COLS), lambda b, i: (b, i, 0))] + [full(a) for a in params],
        out_specs=(pl.BlockSpec((1, ct, prm["w_rwkv_out_bf"].shape[1]), lambda b, i: (b, i, 0)),
                   pl.BlockSpec((1, n_pairs, LANES, LANES), lambda b, i: (b, 0, 0, 0))),
        scratch_shapes=[pltpu.VMEM((8, R_COLS), F32), pltpu.VMEM((n_pairs, LANES, LANES), F32)],
        compiler_params=_cparams(("parallel", "arbitrary")),
        name="rwkv_prompt",
    )(z_r, *params)
    st = st.reshape(B, n_pairs, 2, HEAD_DIM, 2, HEAD_DIM)
    st = jnp.stack([st[:, :, 0, :, 0, :], st[:, :, 1, :, 1, :]], axis=2)
    return y, jnp.swapaxes(st.reshape(B, R_HEADS, HEAD_DIM, HEAD_DIM), -1, -2)


def _rwkv_sample_kernel(z_ref, sh_ref, s_ref, mu_ref, w0_ref, wup_ref, a0_ref, aup_ref, gup_ref,
                        kk_ref, ka_ref, rk_ref, lng_ref, lnb_ref, o_ref, sn_ref):
    ones = _head_ones()
    zr = z_ref[0]
    r, k2, v, kk, a, g, logdec = _rwkv_prep(
        zr, sh_ref[0], mu_ref[...], w0_ref[...], wup_ref[...], a0_ref[...], aup_ref[...],
        gup_ref[...], kk_ref[...], ka_ref[...], ones)
    w = jnp.exp(logdec)
    b = kk * a
    eye = _iota((HEAD_DIM, HEAD_DIM), 0) == _iota((HEAD_DIM, HEAD_DIM), 1)
    col = lambda x: jnp.sum(jnp.where(eye, x, 0.0), axis=1, keepdims=True)
    o_heads = []
    for h in range(R_HEADS):
        hs = slice(h * HEAD_DIM, (h + 1) * HEAD_DIM)
        s = s_ref[0, h]
        sa = jnp.sum(s * (-kk[:, hs]), axis=1, keepdims=True)
        s_new = s * w[:, hs] + sa * b[:, hs] + col(v[:, hs]) * k2[:, hs]
        sn_ref[0, h] = s_new
        o_col = jnp.sum(s_new * r[:, hs], axis=1, keepdims=True)
        o_heads.append(jnp.sum(jnp.where(eye, o_col, 0.0), axis=0, keepdims=True))
    o = jnp.concatenate(o_heads, axis=1)
    o_ref[0] = _rwkv_post(o, r, k2, v, g, rk_ref[...], lng_ref[...], lnb_ref[...], ones)


def _rwkv_sample(z_r, shift, wkv, prm):
    B = z_r.shape[0]
    row = lambda x: x.reshape(1, -1)
    full = lambda a: pl.BlockSpec(a.shape, lambda b: (0,) * a.ndim)
    params = [row(prm["rwkv_mu"]), row(prm["rwkv_w0"]), prm["rwkv_w_up"], row(prm["rwkv_a0"]),
              prm["rwkv_a_up"], prm["rwkv_g_up"], row(prm["rwkv_k_k"]), row(prm["rwkv_k_a"]),
              row(prm["rwkv_r_k"]), row(prm["rwkv_ln_g"]), row(prm["rwkv_ln_b"])]
    o, s_new = pl.pallas_call(
        _rwkv_sample_kernel,
        out_shape=(jax.ShapeDtypeStruct((B, 1, R_WIDTH), F32),
                   jax.ShapeDtypeStruct(wkv.shape, F32)),
        grid=(B,),
        in_specs=[pl.BlockSpec((1, 1, R_COLS), lambda b: (b, 0, 0)),
                  pl.BlockSpec((1, 1, R_COLS), lambda b: (b, 0, 0)),
                  pl.BlockSpec((1,) + wkv.shape[1:], lambda b: (b, 0, 0, 0))] + [full(a) for a in params],
        out_specs=(pl.BlockSpec((1, 1, R_WIDTH), lambda b: (b, 0, 0)),
                   pl.BlockSpec((1,) + wkv.shape[1:], lambda b: (b, 0, 0, 0))),
        compiler_params=_cparams(("parallel",)),
        name="rwkv_sample",
    )(z_r.reshape(B, 1, R_COLS), shift.reshape(B, 1, R_COLS), wkv, *params)
    return o.reshape(B, R_WIDTH), s_new


def _conv_prompt_kernel(z_ref, dw_ref, dwb_ref, lng_ref, lnb_ref, wout_ref, y_ref, nb_ref, u_scr, sh_scr,
                        acc_scr, *, t):
    i = pl.program_id(1)
    pad = 32

    @pl.when(i == 0)
    def _():
        u_scr[0:pad, :] = jnp.zeros((pad, CONV_CH), F32)
        u_scr[pad + t:, :] = jnp.zeros((SUBLANES, CONV_CH), F32)

    z = z_ref[0]
    u_scr[pad:pad + t, :] = z[:, :CONV_CH] * _sigmoid(z[:, CONV_CH:])
    span = t + pad
    for s in range(SUBLANES):
        sh_scr[s, 0:span, :] = u_scr[s:s + span, :]
    off = pad - (CONV_K - 1)

    def rows_block(bi, carry):
        r0 = pl.multiple_of(bi * CONV_ROWS, CONV_ROWS)
        acc = jnp.zeros((CONV_ROWS, CONV_CH), F32) + dwb_ref[...]
        for j in range(CONV_K):
            a, s = divmod(off + j, SUBLANES)
            acc = acc + dw_ref[j:j + 1, :] * sh_scr[s, pl.ds(r0 + a * SUBLANES, CONV_ROWS), :]
        acc_scr[pl.ds(r0, CONV_ROWS), :] = acc
        return carry

    lax.fori_loop(0, t // CONV_ROWS, rows_block, 0)
    tail = u_scr[t:t + pad, :]
    nb_ref[0] = tail
    u_scr[0:pad, :] = tail
    c = _layer_norm(acc_scr[...], lng_ref[...], lnb_ref[...])
    c = c * _sigmoid(c)
    y_ref[0] = _dot(c, wout_ref[...])


def _conv_prompt(z_c, prm, *, t=512):
    B, L, _ = z_c.shape
    t = min(t, L)
    assert L % t == 0 and t >= 32 and t % CONV_ROWS == 0
    row = lambda x: x.reshape(1, -1)
    full = lambda a: pl.BlockSpec(a.shape, lambda b, i: (0,) * a.ndim)
    dw = jnp.pad(prm["conv_dw"], ((0, 32 - CONV_K), (0, 0)))
    params = [dw, row(prm["conv_dw_b"]), row(prm["conv_ln_g"]), row(prm["conv_ln_b"]), prm["w_conv_out_bf"]]
    d_out = prm["w_conv_out_bf"].shape[1]
    y, nb = pl.pallas_call(
        functools.partial(_conv_prompt_kernel, t=t),
        out_shape=(jax.ShapeDtypeStruct((B, L, d_out), F32), jax.ShapeDtypeStruct((B, 32, CONV_CH), F32)),
        grid=(B, L // t),
        in_specs=[pl.BlockSpec((1, t, 2 * CONV_CH), lambda b, i: (b, i, 0))] + [full(a) for a in params],
        out_specs=(pl.BlockSpec((1, t, d_out), lambda b, i: (b, i, 0)),
                   pl.BlockSpec((1, 32, CONV_CH), lambda b, i: (b, 0, 0))),
        scratch_shapes=[pltpu.VMEM((32 + t + SUBLANES, CONV_CH), F32),
                        pltpu.VMEM((SUBLANES, 32 + t, CONV_CH), F32),
                        pltpu.VMEM((t, CONV_CH), F32)],
        compiler_params=_cparams(("parallel", "arbitrary")),
        name="conv_prompt",
    )(z_c, *params)
    return y, nb[:, 32 - (CONV_K - 1):]


def _conv_sample_kernel(z_ref, buf_ref, dw_ref, dwb_ref, lng_ref, lnb_ref, c_ref, u_ref):
    z = z_ref[...]
    u = z[:, :CONV_CH] * _sigmoid(z[:, CONV_CH:])
    u_ref[...] = u
    acc = dwb_ref[...] + dw_ref[CONV_K - 1:CONV_K, :] * u
    for j in range(CONV_K - 1):
        acc = acc + dw_ref[j:j + 1, :] * buf_ref[:, j, :]
    c = _layer_norm(acc, lng_ref[...], lnb_ref[...])
    c_ref[...] = c * _sigmoid(c)


def _conv_sample(z_c, buf, prm):
    B = z_c.shape[0]
    row = lambda x: x.reshape(1, -1)
    args = [z_c, buf, prm["conv_dw"], row(prm["conv_dw_b"]), row(prm["conv_ln_g"]), row(prm["conv_ln_b"])]
    return pl.pallas_call(
        _conv_sample_kernel,
        out_shape=(jax.ShapeDtypeStruct((B, CONV_CH), F32), jax.ShapeDtypeStruct((B, CONV_CH), F32)),
        compiler_params=pltpu.CompilerParams(vmem_limit_bytes=VMEM_LIMIT),
        name="conv_sample",
    )(*args)


def _attn_group(q_ref, k_ref, v_ref, o_ref, m_scr, l_scr, acc_scr, *, seq, dil, nk, first, last):
    nb = (seq // dil) // Q_BLOCK
    qi = _iota((Q_BLOCK, 2 * Q_BLOCK), 0)
    kj = _iota((Q_BLOCK, 2 * Q_BLOCK), 1)
    dist = qi + Q_BLOCK - kj
    in_win = (dist >= 0) & (dist <= nk)
    is_cur = kj >= Q_BLOCK
    lane_lo = _iota((Q_BLOCK, LANES), 1) < HEAD_DIM

    def rows(start):
        return pl.ds(start, Q_BLOCK) if dil == 1 else pl.ds(start, Q_BLOCK, stride=dil)

    def body(it, carry):
        blocks = []
        for u in range(ATTN_UNROLL):
            blk = it * ATTN_UNROLL + u
            d = blk // nb
            n = blk - d * nb
            q0 = d + n * (Q_BLOCK * dil)
            p0 = jnp.maximum(q0 - Q_BLOCK * dil, d)
            blocks.append((q0, p0, in_win & (is_cur | (n > 0))))
        qb = [q_ref[0, rows(q0), :] * ATTN_SCALE for q0, _, _ in blocks]
        kb = [jnp.concatenate([k_ref[0, rows(p0), :], k_ref[0, rows(q0), :]], axis=0).astype(BF16)
              for q0, p0, _ in blocks]
        vb = [jnp.concatenate([v_ref[0, rows(p0), :], v_ref[0, rows(q0), :]], axis=0).astype(BF16)
              for q0, p0, _ in blocks]
        units = [(u, lo) for u in range(ATTN_UNROLL) for lo in (True, False)]
        s = [_dot_nt(jnp.where(lane_lo == lo, qb[u], 0.0), kb[u]) for u, lo in units]
        s = [jnp.where(blocks[u][2], x, NEG_BIG) for (u, _), x in zip(units, s)]
        m = [jnp.max(x, axis=1, keepdims=True) for x in s]
        p = [jnp.exp(x - mm) for x, mm in zip(s, m)]
        l = [jnp.sum(x, axis=1, keepdims=True) for x in p]
        o = [jnp.dot(x.astype(BF16), vb[u], preferred_element_type=F32) for (u, _), x in zip(units, p)]
        for u in range(ATTN_UNROLL):
            m_b = jnp.where(lane_lo, m[2 * u], m[2 * u + 1])
            l_b = jnp.where(lane_lo, l[2 * u], l[2 * u + 1])
            o_b = jnp.where(lane_lo, o[2 * u], o[2 * u + 1])
            r = rows(blocks[u][0])
            if not first:
                m_old = m_scr[r, :]
                m_new = jnp.maximum(m_old, m_b)
                a_old = jnp.exp(m_old - m_new)
                a_new = jnp.exp(m_b - m_new)
                l_b = l_scr[r, :] * a_old + l_b * a_new
                o_b = acc_scr[r, :] * a_old + o_b * a_new
                m_b = m_new
            if last:
                o_ref[0, r, :] = o_b / l_b
            else:
                m_scr[r, :] = m_b
                l_scr[r, :] = l_b
                acc_scr[r, :] = o_b
        return carry

    lax.fori_loop(0, seq // (Q_BLOCK * ATTN_UNROLL), body, 0)


def _attn_prompt_kernel(q_ref, k_ref, v_ref, o_ref, m_scr, l_scr, acc_scr, *, seq):
    g = pl.program_id(2)
    for gi, (win, dil) in enumerate(SWA_GROUPS):
        @pl.when(g == gi)
        def _(dil=dil, nk=win // dil, gi=gi):
            _attn_group(q_ref, k_ref, v_ref, o_ref, m_scr, l_scr, acc_scr, seq=seq, dil=dil, nk=nk,
                        first=gi == 0, last=gi == N_GROUPS - 1)


def _attn_prompt(z_qkv):
    B, S, _ = z_qkv.shape
    for win, dil in SWA_GROUPS:
        assert S % (dil * Q_BLOCK) == 0 and win // dil <= Q_BLOCK
    assert (S // Q_BLOCK) % ATTN_UNROLL == 0
    pairs = G_WIDTH // LANES
    blk = lambda base: pl.BlockSpec((1, S, LANES), lambda b, p, g: (b, 0, base + g * pairs + p))
    return pl.pallas_call(
        functools.partial(_attn_prompt_kernel, seq=S),
        out_shape=jax.ShapeDtypeStruct((B, S, G_WIDTH), F32),
        grid=(B, pairs, N_GROUPS),
        in_specs=[blk(0), blk(A_WIDTH // LANES), blk(2 * A_WIDTH // LANES)],
        out_specs=pl.BlockSpec((1, S, LANES), lambda b, p, g: (b, 0, p)),
        scratch_shapes=[pltpu.VMEM((S, LANES), F32)] * 3,
        compiler_params=_cparams(("parallel", "parallel", "arbitrary")),
        name="attn_prompt",
    )(z_qkv, z_qkv, z_qkv)


def _attn_sample_kernel(q_ref, kn_ref, vn_ref, ca_ref, cb_ref, cc_ref, o_ref, *, bs):
    caches = (ca_ref, cb_ref, cc_ref)
    hrow = _iota((8, G_WIDTH), 0)
    hmask = hrow == (_iota((8, G_WIDTH), 1) >> 6)
    outs = []
    for n in range(bs):
        m_run = l_run = acc = None
        for gi in range(N_GROUPS):
            gs = slice(gi * G_WIDTH, (gi + 1) * G_WIDTH)
            q = q_ref[n:n + 1, gs] * ATTN_SCALE
            qr = jnp.where(hmask, q, 0.0).astype(BF16)
            kc = caches[gi][n, :, 0:G_WIDTH].astype(BF16)
            vc = caches[gi][n, :, G_WIDTH:2 * G_WIDTH].astype(BF16)
            kn = kn_ref[n:n + 1, gs].astype(BF16).astype(F32)
            vn = vn_ref[n:n + 1, gs].astype(BF16).astype(F32)
            s = _dot_nt(qr, kc)
            s_n = jnp.sum(qr.astype(F32) * kn, axis=1, keepdims=True)
            m = jnp.maximum(jnp.max(s, axis=1, keepdims=True), s_n)
            p = jnp.exp(s - m)
            p_n = jnp.exp(s_n - m)
            l = jnp.sum(p, axis=1, keepdims=True) + p_n
            o = jnp.dot(p.astype(BF16), vc, preferred_element_type=F32) + p_n.astype(BF16).astype(F32) * vn
            if m_run is None:
                m_run, l_run, acc = m, l, o
            else:
                m_new = jnp.maximum(m_run, m)
                a_old = jnp.exp(m_run - m_new)
                a_new = jnp.exp(m - m_new)
                l_run = l_run * a_old + l * a_new
                acc = acc * a_old + o * a_new
                m_run = m_new
        outs.append(jnp.sum(jnp.where(hmask, acc / l_run, 0.0), axis=0, keepdims=True))
    o_ref[...] = jnp.concatenate(outs, axis=0)


def _attn_sample(q, k_new, v_new, strided, *, bs=8):
    B = q.shape[0]
    rows = [c.shape[1] for c in strided]
    vec = pl.BlockSpec((bs, A_WIDTH), lambda i: (i, 0))
    return pl.pallas_call(
        functools.partial(_attn_sample_kernel, bs=bs),
        out_shape=jax.ShapeDtypeStruct((B, G_WIDTH), F32),
        grid=(B // bs,),
        in_specs=[vec, vec, vec] + [pl.BlockSpec((bs, r, 2 * G_WIDTH), lambda i: (i, 0, 0)) for r in rows],
        out_specs=pl.BlockSpec((bs, G_WIDTH), lambda i: (i, 0)),
        compiler_params=_cparams(("parallel",)),
        name="attn_sample",
    )(q, k_new, v_new, *strided)


def _merge_kernel(zg_ref, yr_ref, yc_ref, ao_ref, x_ref, gb_ref, wa_ref, wo_ref, g_ref, b_ref, h_ref, *,
                  alpha):
    d = x_ref.shape[1]
    zg = zg_ref[...]
    gb = gb_ref[...]
    gate = lambda i: _sigmoid(zg[:, i * d:(i + 1) * d] + gb[:, i * d:(i + 1) * d])
    y_a = _dot(ao_ref[...], wa_ref[...])
    merged = gate(0) * yr_ref[...] + gate(1) * yc_ref[...] + gate(2) * y_a
    h_ref[...] = _layer_norm(alpha * x_ref[...] + _dot(merged, wo_ref[...]), g_ref[...], b_ref[...])


def _merge(z_g, y_r, y_c, a_o, x, prm, *, alpha, t=512):
    M, D = x.shape
    t = min(t, M)
    assert M % t == 0
    row = lambda a: a.reshape(1, -1)
    tile = lambda a: pl.BlockSpec((t, a.shape[1]), lambda i: (i, 0))
    full = lambda a: pl.BlockSpec(a.shape, lambda i: (0, 0))
    acts = [z_g, y_r, y_c, a_o, x]
    params = [row(prm["gate_b"]), prm["w_attn_out_bf"], prm["w_o_bf"], row(prm["ln1_g"]), row(prm["ln1_b"])]
    return pl.pallas_call(
        functools.partial(_merge_kernel, alpha=alpha),
        out_shape=jax.ShapeDtypeStruct((M, D), F32),
        grid=(M // t,),
        in_specs=[tile(a) for a in acts] + [full(a) for a in params],
        out_specs=pl.BlockSpec((t, D), lambda i: (i, 0)),
        compiler_params=_cparams(("parallel",)),
        name="merge",
    )(*acts, *params)


def _ffn_kernel(*refs, alpha, t, tiles_per_seq, sample):
    if sample:
        (h_ref, pg_ref, pu_ref, wg_ref, wu_ref, dg_ref, du_ref, bg_ref, bu_ref, wo_ref, g_ref, b_ref,
         o_ref, ng_ref, nu_ref, acc_scr) = refs
    else:
        (h_ref, wg_ref, wu_ref, dg_ref, du_ref, bg_ref, bu_ref, wo_ref, g_ref, b_ref,
         o_ref, ng_ref, nu_ref, acc_scr, cg_scr, cu_scr) = refs
    i = pl.program_id(0)
    j = pl.program_id(1)
    nj = pl.num_programs(1)
    hb = h_ref[...].astype(BF16)

    def conv(u, dw_ref, bias_ref, p1, p2):
        return dw_ref[0:1, :] * p2 + dw_ref[1:2, :] * p1 + dw_ref[2:3, :] * u + bias_ref[...]

    def branch(w_ref, dw_ref, bias_ref, new_ref, prev_ref_or_scr):
        u = jnp.dot(hb, w_ref[...], preferred_element_type=F32)
        if sample:
            new_ref[...] = u
            p2, p1 = prev_ref_or_scr[0], prev_ref_or_scr[1]
        else:
            new_ref[0] = u[t - 2:t, :]
            row = _iota(u.shape, 0)
            fresh = (i % tiles_per_seq) == 0
            c1 = jnp.where(fresh, 0.0, prev_ref_or_scr[j, 7:8, :])
            c2 = jnp.where(fresh, 0.0, prev_ref_or_scr[j, 6:7, :])
            p1 = jnp.where(row == 0, c1, pltpu.roll(u, 1, 0))
            p2 = jnp.where(row == 0, c2, jnp.where(row == 1, c1, pltpu.roll(u, 2, 0)))
            prev_ref_or_scr[j, :, :] = u[t - 8:t, :]
        return conv(u, dw_ref, bias_ref, p1, p2)

    if sample:
        y_g = branch(wg_ref, dg_ref, bg_ref, ng_ref, pg_ref)
        y_u = branch(wu_ref, du_ref, bu_ref, nu_ref, pu_ref)
    else:
        y_g = branch(wg_ref, dg_ref, bg_ref, ng_ref, cg_scr)
        y_u = branch(wu_ref, du_ref, bu_ref, nu_ref, cu_scr)
    act = (y_g * _sigmoid(y_g)) * y_u
    part = _dot(act, wo_ref[...])

    @pl.when(j == 0)
    def _():
        acc_scr[...] = part

    @pl.when(j > 0)
    def _():
        acc_scr[...] += part

    @pl.when(j == nj - 1)
    def _():
        o_ref[...] = _layer_norm(alpha * h_ref[...] + acc_scr[...], g_ref[...], b_ref[...])


def _ffn(h, prm, *, alpha, seq_len, prev=None, t=512, tn=1408):
    M, D = h.shape
    d_ff = prm["w_ffn_out_bf"].shape[0]
    sample = prev is not None
    t = M if sample else min(t, seq_len)
    tn = min(tn, d_ff)
    assert M % t == 0 and d_ff % tn == 0 and (sample or (seq_len % t == 0 and t >= 8))
    J = d_ff // tn
    row = lambda a: a.reshape(1, -1)
    w_in, dw, dwb = prm["w_ffn_in_bf"], prm["ffn_dw"], row(prm["ffn_dw_b"])
    col_g = lambda shape: pl.BlockSpec(shape, lambda i, j: (0, j))
    col_u = lambda shape: pl.BlockSpec(shape, lambda i, j: (0, J + j))
    const = lambda a: pl.BlockSpec(a.shape, lambda i, j: (0, 0))
    ins, specs = [h], [pl.BlockSpec((t, D), lambda i, j: (i, 0))]
    if sample:
        pstack = jnp.stack(prev)
        ins += [pstack, pstack]
        specs += [pl.BlockSpec((2, M, tn), lambda i, j: (0, 0, j)), pl.BlockSpec((2, M, tn), lambda i, j: (0, 0, J + j))]
    ins += [w_in, w_in, dw, dw, dwb, dwb, prm["w_ffn_out_bf"], row(prm["ln2_g"]), row(prm["ln2_b"])]
    specs += [col_g((D, tn)), col_u((D, tn)), col_g((FFN_K, tn)), col_u((FFN_K, tn)), col_g((1, tn)),
              col_u((1, tn)), pl.BlockSpec((tn, D), lambda i, j: (j, 0)), const(row(prm["ln2_g"])),
              const(row(prm["ln2_b"]))]
    if sample:
        new_shape = jax.ShapeDtypeStruct((M, d_ff), F32)
        new_spec = pl.BlockSpec((M, tn), lambda i, j: (0, j))
        scratch = [pltpu.VMEM((t, D), F32)]
        tiles_per_seq = 1
    else:
        tiles_per_seq = seq_len // t
        new_shape = jax.ShapeDtypeStruct((M // t, 2, d_ff), F32)
        new_spec = pl.BlockSpec((1, 2, tn), lambda i, j: (i, 0, j))
        scratch = [pltpu.VMEM((t, D), F32), pltpu.VMEM((J, 8, tn), F32), pltpu.VMEM((J, 8, tn), F32)]
    out, new_g, new_u = pl.pallas_call(
        functools.partial(_ffn_kernel, alpha=alpha, t=t, tiles_per_seq=tiles_per_seq, sample=sample),
        out_shape=(jax.ShapeDtypeStruct((M, D), F32), new_shape, new_shape),
        grid=(M // t, J),
        in_specs=specs,
        out_specs=(pl.BlockSpec((t, D), lambda i, j: (i, 0)), new_spec, new_spec),
        scratch_shapes=scratch,
        compiler_params=_cparams(("arbitrary", "arbitrary")),
        name="ffn_sample" if sample else "ffn_prompt",
    )(*ins)
    if not sample:
        new_g = new_g[tiles_per_seq - 1::tiles_per_seq]
        new_u = new_u[tiles_per_seq - 1::tiles_per_seq]
    return out, new_g, new_u


def _layer_weights(l, w_in, w_rwkv_out, w_conv_out, w_attn_out, w_o, w_ffn_in, w_ffn_out, small):
    prm = {k: v[l] for k, v in small.items()}
    wi = w_in[l]
    o_c = R_COLS
    o_q = o_c + 2 * CONV_CH
    o_g = o_q + 3 * A_WIDTH
    prm["w_in_r"] = wi[:, :o_c].astype(BF16)
    prm["w_in_c"] = wi[:, o_c:o_q].astype(BF16)
    prm["w_in_qkv"] = wi[:, o_q:o_g].astype(BF16)
    prm["w_in_g"] = wi[:, o_g:].astype(BF16)
    prm["w_rwkv_out_bf"] = w_rwkv_out[l].astype(BF16)
    prm["w_conv_out_bf"] = w_conv_out[l].astype(BF16)
    prm["w_attn_out_bf"] = w_attn_out[l].astype(BF16)
    prm["w_o_bf"] = w_o[l].astype(BF16)
    prm["w_ffn_in_bf"] = w_ffn_in[l].astype(BF16)
    prm["w_ffn_out_bf"] = w_ffn_out[l].astype(BF16)
    return prm


def _project(x2, prm, tm):
    z_r = _mm(x2, prm["w_in_r"], tm=tm, tn=R_COLS)
    z_c = _mm(x2, prm["w_in_c"], tm=tm, tn=2 * CONV_CH)
    z_qkv = _mm(x2, prm["w_in_qkv"], tm=tm, tn=3 * A_WIDTH // 2)
    z_g = _mm(x2, prm["w_in_g"], tm=tm, tn=1024)
    return z_r, z_c, z_qkv, z_g


def _kv_rows(z_qkv, gi):
    k = z_qkv[..., A_WIDTH + gi * G_WIDTH:A_WIDTH + (gi + 1) * G_WIDTH]
    v = z_qkv[..., 2 * A_WIDTH + gi * G_WIDTH:2 * A_WIDTH + (gi + 1) * G_WIDTH]
    kv = jnp.stack([k, v], axis=-2)
    return kv.reshape(kv.shape[:-1] + (G_HEADS, HEAD_DIM))


def _prompt_layer(x, prm, alpha):
    B, L, D = x.shape
    x2 = x.reshape(B * L, D)
    z_r, z_c, z_qkv, z_g = _project(x2, prm, tm=1024)
    z_r = z_r.reshape(B, L, -1)
    z_qkv = z_qkv.reshape(B, L, -1)
    y_r, new_wkv = _rwkv_prompt(z_r, prm)
    y_c, new_conv = _conv_prompt(z_c.reshape(B, L, -1), prm)
    a_o = _attn_prompt(z_qkv)
    h = _merge(z_g, y_r.reshape(B * L, D), y_c.reshape(B * L, D), a_o.reshape(B * L, -1), x2, prm, alpha=alpha)
    out, nf_g, nf_u = _ffn(h, prm, alpha=alpha, seq_len=L)
    new_kv = [_kv_rows(z_qkv[:, L - min(win, L):], gi) for gi, (win, _) in enumerate(SWA_GROUPS)]
    return (out.reshape(B, L, D), z_r[:, L - 1:], new_wkv, new_conv, new_kv,
            jnp.concatenate([nf_g, nf_u], axis=-1))


def _sample_layer(x, prm, alpha, shift, wkv, conv_buf, ffn_buf, kv_strided):
    B, L, D = x.shape
    assert L == 1
    x2 = x.reshape(B, D)
    z_r, z_c, z_qkv, z_g = _project(x2, prm, tm=B)
    o_r, new_wkv = _rwkv_sample(z_r, shift.reshape(B, R_COLS), wkv, prm)
    y_r = _mm(o_r, prm["w_rwkv_out_bf"], tm=B, tn=1024)
    c, u = _conv_sample(z_c, conv_buf, prm)
    y_c = _mm(c, prm["w_conv_out_bf"], tm=B, tn=1024)
    a_o = _attn_sample(z_qkv[:, :A_WIDTH], z_qkv[:, A_WIDTH:2 * A_WIDTH], z_qkv[:, 2 * A_WIDTH:], kv_strided)
    h = _merge(z_g, y_r, y_c, a_o, x2, prm, alpha=alpha)
    out, u_g, u_u = _ffn(h, prm, alpha=alpha, seq_len=1, prev=(ffn_buf[:, 0], ffn_buf[:, 1]))
    kv_rows = [_kv_rows(z_qkv, gi) for gi in range(N_GROUPS)]
    return (out.reshape(B, 1, D), z_r.reshape(B, 1, R_COLS), new_wkv, u, kv_rows,
            jnp.concatenate([u_g, u_u], axis=-1))


def _append_row(buf, rows):
    return jnp.concatenate([buf[:, :, 1:], rows[:, :, None]], axis=2)


def kernel(x_prompt, x_sample, state_shift, state_wkv, state_conv, cache_swa_a, cache_swa_b, cache_swa_c, state_ffn, w_in, rwkv_mu, rwkv_w0, rwkv_w_up, rwkv_a0, rwkv_a_up, rwkv_g_up, rwkv_k_k, rwkv_k_a, rwkv_r_k, rwkv_ln_g, rwkv_ln_b, w_rwkv_out, conv_dw, conv_dw_b, conv_ln_g, conv_ln_b, w_conv_out, w_attn_out, gate_b, w_o, ln1_g, ln1_b, w_ffn_in, ffn_dw, ffn_dw_b, w_ffn_out, ln2_g, ln2_b):
    depth = w_in.shape[0]
    alpha = (2 * depth) ** 0.25
    small = dict(rwkv_mu=rwkv_mu, rwkv_w0=rwkv_w0, rwkv_w_up=rwkv_w_up, rwkv_a0=rwkv_a0, rwkv_a_up=rwkv_a_up,
                 rwkv_g_up=rwkv_g_up, rwkv_k_k=rwkv_k_k, rwkv_k_a=rwkv_k_a,
                 rwkv_r_k=rwkv_r_k.reshape(depth, -1), rwkv_ln_g=rwkv_ln_g, rwkv_ln_b=rwkv_ln_b,
                 conv_dw=conv_dw, conv_dw_b=conv_dw_b, conv_ln_g=conv_ln_g, conv_ln_b=conv_ln_b,
                 gate_b=gate_b.reshape(depth, -1), ln1_g=ln1_g, ln1_b=ln1_b, ffn_dw=ffn_dw, ffn_dw_b=ffn_dw_b,
                 ln2_g=ln2_g, ln2_b=ln2_b)
    caches = (cache_swa_a, cache_swa_b, cache_swa_c)
    bs = x_sample.shape[0]
    for (win, dil), c in zip(SWA_GROUPS, caches):
        assert c.shape[2] == win and win % dil == 0
    hp, hs = x_prompt, x_sample
    outs_p, outs_s = [], []
    for l in range(depth):
        prm = _layer_weights(l, w_in, w_rwkv_out, w_conv_out, w_attn_out, w_o, w_ffn_in, w_ffn_out, small)
        res_p = _prompt_layer(hp, prm, alpha)
        hp = res_p[0]
        outs_p.append(res_p[1:])
        kv_strided = [c[l, :, ::dil].reshape(bs, -1, 2 * G_WIDTH) for (_, dil), c in zip(SWA_GROUPS, caches)]
        res_s = _sample_layer(hs, prm, alpha, state_shift[l], state_wkv[l], state_conv[l], state_ffn[l],
                              kv_strided)
        hs = res_s[0]
        outs_s.append(res_s[1:])
    stk = lambda outs, f: jnp.stack([f(o) for o in outs])
    res = [hp, hs]
    res += [stk(outs_p, lambda o: o[0]), stk(outs_s, lambda o: o[0])]
    res += [stk(outs_p, lambda o: o[1]), stk(outs_s, lambda o: o[1])]
    res += [stk(outs_p, lambda o: o[2]), _append_row(state_conv, stk(outs_s, lambda o: o[2]))]
    for gi in range(N_GROUPS):
        res += [stk(outs_p, lambda o: o[3][gi]), _append_row(caches[gi], stk(outs_s, lambda o: o[3][gi]))]
    res += [stk(outs_p, lambda o: o[4]), _append_row(state_ffn, stk(outs_s, lambda o: o[4]))]
    return tuple(res)
```

```python
import functools

import jax
import jax.numpy as jnp
from jax import lax
from jax.experimental import pallas as pl
from jax.experimental.pallas import tpu as pltpu

F32 = jnp.float32
BF16 = jnp.bfloat16

HEAD_DIM = 64
R_HEADS = 8
R_WIDTH = R_HEADS * HEAD_DIM
DECAY_LORA = 64
ICLR_LORA = 64
GATE_LORA = 128
R_COLS = 3 * R_WIDTH + DECAY_LORA + ICLR_LORA + GATE_LORA
LNX_EPS = 64e-5
CONV_CH = 512
CONV_K = 31
SWA_GROUPS = ((128, 1), (512, 4), (2048, 16))
N_GROUPS = len(SWA_GROUPS)
G_HEADS = 4
G_WIDTH = G_HEADS * HEAD_DIM
A_WIDTH = N_GROUPS * G_WIDTH
Q_BLOCK = 128
ATTN_SCALE = HEAD_DIM ** -0.5
N_BRANCH = 3
FFN_K = 3
LN_EPS = 1e-5
NEG_BIG = -1e30

LANES = 128
CHUNK = 64
SUBLANES = 8
CONV_ROWS = 32
SEG_BLOCK = 256
ATTN_UNROLL = 4
VMEM_LIMIT = 56 * 1024 * 1024


def _cparams(sem):
    return pltpu.CompilerParams(dimension_semantics=sem, vmem_limit_bytes=VMEM_LIMIT)


def _dot(a, b):
    return jnp.dot(a.astype(BF16), b.astype(BF16), preferred_element_type=F32)


def _dot_nt(a, b):
    return lax.dot_general(a.astype(BF16), b.astype(BF16), (((1,), (1,)), ((), ())),
                           preferred_element_type=F32)


def _dot_tn(a, b):
    return lax.dot_general(a.astype(BF16), b.astype(BF16), (((0,), (0,)), ((), ())),
                           preferred_element_type=F32)


def _split3(x):
    h1 = x.astype(BF16)
    r1 = x - h1.astype(F32)
    h2 = r1.astype(BF16)
    h3 = (r1 - h2.astype(F32)).astype(BF16)
    return h1, h2, h3


def _sigmoid(x):
    return 1.0 / (1.0 + jnp.exp(-x))


def _layer_norm(x, g, b):
    mu = jnp.mean(x, axis=-1, keepdims=True)
    xc = x - mu
    var = jnp.mean(xc * xc, axis=-1, keepdims=True)
    return xc * lax.rsqrt(var + LN_EPS) * g + b


def _iota(shape, dim):
    return lax.broadcasted_iota(jnp.int32, shape, dim)


def _head_ones():
    r = _iota((SEG_BLOCK, SEG_BLOCK), 0)
    c = _iota((SEG_BLOCK, SEG_BLOCK), 1)
    return jnp.where((r >> 6) == (c >> 6), 1.0, 0.0).astype(BF16)


def _segsum(x, ones):
    outs = [_dot(x[:, p:p + SEG_BLOCK], ones) for p in range(0, x.shape[1], SEG_BLOCK)]
    return outs[0] if len(outs) == 1 else jnp.concatenate(outs, axis=1)


def _mm_kernel(x_ref, w_ref, o_ref):
    o_ref[...] = jnp.dot(x_ref[...].astype(BF16), w_ref[...],
                         preferred_element_type=F32).astype(o_ref.dtype)


def _mm_gate_kernel(x_ref, w_ref, b_ref, o_ref):
    z = jnp.dot(x_ref[...].astype(BF16), w_ref[...], preferred_element_type=F32)
    o_ref[...] = _sigmoid(z + b_ref[...]).astype(o_ref.dtype)


def _mm(x, w, *, tm, tn, out_dtype=F32, gate_bias=None):
    M, K = x.shape
    N = w.shape[1]
    tm = min(tm, M)
    tn = min(tn, N)
    assert M % tm == 0 and N % tn == 0
    ins = [x, w]
    specs = [pl.BlockSpec((tm, K), lambda i, j: (i, 0)), pl.BlockSpec((K, tn), lambda i, j: (0, j))]
    if gate_bias is not None:
        ins.append(gate_bias)
        specs.append(pl.BlockSpec((1, tn), lambda i, j: (0, j)))
    return pl.pallas_call(
        _mm_kernel if gate_bias is None else _mm_gate_kernel,
        out_shape=jax.ShapeDtypeStruct((M, N), out_dtype),
        grid=(M // tm, N // tn),
        in_specs=specs,
        out_specs=pl.BlockSpec((tm, tn), lambda i, j: (i, j)),
        compiler_params=_cparams(("parallel", "parallel")),
        name="mm" if gate_bias is None else "mm_gate",
    )(*ins)


def _rwkv_prep(zr, prevs, mu, w0, w_up, a0, a_up, g_up, k_k, k_a, ones):
    zs = zr + (prevs - zr) * mu
    r = zs[:, 0:R_WIDTH]
    k = zs[:, R_WIDTH:2 * R_WIDTH]
    v = zs[:, 2 * R_WIDTH:3 * R_WIDTH]
    o1 = 3 * R_WIDTH
    wd = zs[:, o1:o1 + DECAY_LORA]
    ad = zs[:, o1 + DECAY_LORA:o1 + DECAY_LORA + ICLR_LORA]
    gd = zs[:, o1 + DECAY_LORA + ICLR_LORA:R_COLS]
    nx = -(w0 + _dot(jnp.tanh(wd), w_up))
    softplus = jnp.maximum(nx, 0.0) + jnp.log(1.0 + jnp.exp(-jnp.abs(nx)))
    logdec = -jnp.exp(-softplus - 0.5)
    a = _sigmoid(a0 + _dot(ad, a_up))
    g = _dot(_sigmoid(gd), g_up)
    kk = k * k_k
    kk = kk / jnp.maximum(jnp.sqrt(_segsum(kk * kk, ones)), 1e-12)
    k2 = k * (1.0 + (a - 1.0) * k_a)
    return r, k2, v, kk, a, g, logdec


def _rwkv_post(o, r, k2, v, g, r_k, ln_g, ln_b, ones):
    inv = 1.0 / HEAD_DIM
    m = _segsum(o, ones) * inv
    oc = o - m
    var = _segsum(oc * oc, ones) * inv
    on = oc * lax.rsqrt(var + LNX_EPS) * ln_g + ln_b
    bonus = _segsum(r * k2 * r_k, ones) * v
    return (on + bonus) * g


def _chunk_cumsum(logdec, tri_bd):
    h1, h2, h3 = _split3(logdec)
    d = lambda x: jnp.dot(tri_bd, x, preferred_element_type=F32)
    return d(h1) + d(h2) + d(h3)


def _rwkv_chunk_tables(logdec, cum_all, r, k2, v, kk, a, c):
    sl = slice(c * CHUNK, (c + 1) * CHUNK)
    cum = cum_all[sl]
    ld = logdec[sl]
    cum_end = cum[CHUNK - 1:CHUNK, :]
    p_in = jnp.exp(cum)
    p_ex = jnp.exp(cum - ld)
    p_inv = jnp.exp(-cum)
    p_tail = jnp.exp(cum_end - cum)
    kka = kk[sl] * a[sl]
    return dict(rt=r[sl] * p_in, at=-kk[sl] * p_ex, bt=kka * p_inv, kt=k2[sl] * p_inv,
                bh=kka * p_tail, kh=k2[sl] * p_tail, v=v[sl], pc=jnp.exp(cum_end))


def _rwkv_units(tables, n_pairs, masks):
    strict, incl, eye, lane_lo = masks
    n = 2 * CHUNK
    units = [(tb, slice(p * LANES, (p + 1) * LANES)) for tb in tables for p in range(n_pairs)]

    def stack(x, ls):
        x = x[:, ls]
        return jnp.concatenate([jnp.where(lane_lo, x, 0.0), jnp.where(lane_lo, 0.0, x)], axis=0)

    ops = [{k: stack(tb[k], ls) for k in ("at", "rt", "bt", "kt", "bh", "kh", "v")} for tb, ls in units]
    aa = [_dot_nt(jnp.concatenate([o["at"], o["rt"]], axis=0), jnp.concatenate([o["bt"], o["kt"]], axis=0))
          for o in ops]
    lj = [jnp.where(strict, a[:n, :n], 0.0) for a in aa]
    x = [jnp.concatenate([o["at"], _dot(jnp.where(strict, a[:n, n:], 0.0), o["v"])], axis=1)
         for o, a in zip(ops, aa)]
    steps = CHUNK.bit_length() - 1
    for j in range(steps):
        if j < steps - 1:
            prod = [_dot(l, jnp.concatenate([l, xx], axis=1)) for l, xx in zip(lj, x)]
            lj = [pr[:, :n] for pr in prod]
            x = [xx + pr[:, n:] for xx, pr in zip(x, prod)]
        else:
            x = [xx + _dot(l, xx) for l, xx in zip(lj, x)]
    rhs = [jnp.concatenate([xx, jnp.concatenate([jnp.zeros_like(o["v"]), o["v"]], axis=1)], axis=0)
           for xx, o in zip(x, ops)]
    qo = [_dot(jnp.concatenate([jnp.where(incl, a[n:, :n], 0.0), jnp.where(incl, a[n:, n:], 0.0)], axis=1), rh)
          for a, rh in zip(aa, rhs)]
    gh = [_dot_tn(jnp.concatenate([o["bh"], o["kh"]], axis=0), rh) for o, rh in zip(ops, rhs)]
    out = []
    for (tb, ls), o, q, g in zip(units, ops, qo, gh):
        out.append((o["rt"] + q[:, :n], q[:, n:], g[:, :n] + jnp.where(eye, tb["pc"][:, ls], 0.0), g[:, n:]))
    return out


def _rwkv_masks():
    n = 2 * CHUNK
    rr = _iota((n, n), 0)
    cc = _iota((n, n), 1)
    same = (rr >> 6) == (cc >> 6)
    strict = same & ((cc & 63) < (rr & 63))
    incl = same & ((cc & 63) <= (rr & 63))
    eye = rr == cc
    lane_lo = _iota((CHUNK, LANES), 1) < HEAD_DIM
    return strict, incl, eye, lane_lo


def _rwkv_prompt_kernel(z_ref, mu_ref, w0_ref, wup_ref, a0_ref, aup_ref, gup_ref, kk_ref, ka_ref,
                        rk_ref, lng_ref, lnb_ref, wout_ref, y_ref, st_ref, prev_scr, st_scr, *, ct):
    i = pl.program_id(1)

    @pl.when(i == 0)
    def _():
        prev_scr[...] = jnp.zeros_like(prev_scr)
        st_scr[...] = jnp.zeros_like(st_scr)

    zr = z_ref[0]
    row = _iota(zr.shape, 0)
    prevs = jnp.where(row == 0, prev_scr[0:1, :], pltpu.roll(zr, 1, 0))
    prev_scr[0:1, :] = zr[ct - 1:ct, :]
    ones = _head_ones()
    r, k2, v, kk, a, g, logdec = _rwkv_prep(
        zr, prevs, mu_ref[...], w0_ref[...], wup_ref[...], a0_ref[...], aup_ref[...], gup_ref[...],
        kk_ref[...], ka_ref[...], ones)

    rr = _iota((ct, ct), 0)
    cc = _iota((ct, ct), 1)
    tri_bd = jnp.where(((rr >> 6) == (cc >> 6)) & (cc <= rr), 1.0, 0.0).astype(BF16)
    masks = _rwkv_masks()
    n_pairs = R_WIDTH // LANES
    cum_all = _chunk_cumsum(logdec, tri_bd)
    n_chunks = ct // CHUNK
    tables = [_rwkv_chunk_tables(logdec, cum_all, r, k2, v, kk, a, c) for c in range(n_chunks)]
    units = _rwkv_units(tables, n_pairs, masks)
    states = [st_scr[p] for p in range(n_pairs)]
    o_rows = []
    for c in range(n_chunks):
        cur = units[c * n_pairs:(c + 1) * n_pairs]
        ostk = [_dot(qp, st) + op for (qp, op, _, _), st in zip(cur, states)]
        states = [_dot(gm, st) + hm for (_, _, gm, hm), st in zip(cur, states)]
        o_rows.append(jnp.concatenate([o[:CHUNK] + o[CHUNK:] for o in ostk], axis=1))
    for p in range(n_pairs):
        st_scr[p] = states[p]
    o = o_rows[0] if len(o_rows) == 1 else jnp.concatenate(o_rows, axis=0)
    out = _rwkv_post(o, r, k2, v, g, rk_ref[...], lng_ref[...], lnb_ref[...], ones)
    y_ref[0] = _dot(out, wout_ref[...]).astype(y_ref.dtype)
    st_ref[0] = st_scr[...]


def _rwkv_prompt(z_r, prm, *, ct=256):
    B, L, _ = z_r.shape
    assert L % ct == 0 and ct % CHUNK == 0
    n_pairs = R_WIDTH // LANES
    row = lambda x: x.reshape(1, -1)
    full = lambda a: pl.BlockSpec(a.shape, lambda b, i: (0,) * a.ndim)
    params = [row(prm["rwkv_mu"]), row(prm["rwkv_w0"]), prm["rwkv_w_up"], row(prm["rwkv_a0"]),
              prm["rwkv_a_up"], prm["rwkv_g_up"], row(prm["rwkv_k_k"]), row(prm["rwkv_k_a"]),
              row(prm["rwkv_r_k"]), row(prm["rwkv_ln_g"]), row(prm["rwkv_ln_b"]), prm["w_rwkv_out_bf"]]
    y, st = pl.pallas_call(
        functools.partial(_rwkv_prompt_kernel, ct=ct),
        out_shape=(jax.ShapeDtypeStruct((B, L, prm["w_rwkv_out_bf"].shape[1]), BF16),
                   jax.ShapeDtypeStruct((B, n_pairs, LANES, LANES), F32)),
        grid=(B, L // ct),
        in_specs=[pl.BlockSpec((1, ct, R_COLS), lambda b, i: (b, i, 0))] + [full(a) for a in params],
        out_specs=(pl.BlockSpec((1, ct, prm["w_rwkv_out_bf"].shape[1]), lambda b, i: (b, i, 0)),
                   pl.BlockSpec((1, n_pairs, LANES, LANES), lambda b, i: (b, 0, 0, 0))),
        scratch_shapes=[pltpu.VMEM((8, R_COLS), F32), pltpu.VMEM((n_pairs, LANES, LANES), F32)],
        compiler_params=_cparams(("parallel", "arbitrary")),
        name="rwkv_prompt",
    )(z_r, *params)
    st = st.reshape(B, n_pairs, 2, HEAD_DIM, 2, HEAD_DIM)
    st = jnp.stack([st[:, :, 0, :, 0, :], st[:, :, 1, :, 1, :]], axis=2)
    return y, jnp.swapaxes(st.reshape(B, R_HEADS, HEAD_DIM, HEAD_DIM), -1, -2)


def _rwkv_sample_kernel(z_ref, sh_ref, s_ref, mu_ref, w0_ref, wup_ref, a0_ref, aup_ref, gup_ref,
                        kk_ref, ka_ref, rk_ref, lng_ref, lnb_ref, o_ref, sn_ref):
    ones = _head_ones()
    zr = z_ref[0]
    r, k2, v, kk, a, g, logdec = _rwkv_prep(
        zr, sh_ref[0], mu_ref[...], w0_ref[...], wup_ref[...], a0_ref[...], aup_ref[...],
        gup_ref[...], kk_ref[...], ka_ref[...], ones)
    w = jnp.exp(logdec)
    b = kk * a
    eye = _iota((HEAD_DIM, HEAD_DIM), 0) == _iota((HEAD_DIM, HEAD_DIM), 1)
    col = lambda x: jnp.sum(jnp.where(eye, x, 0.0), axis=1, keepdims=True)
    o_heads = []
    for h in range(R_HEADS):
        hs = slice(h * HEAD_DIM, (h + 1) * HEAD_DIM)
        s = s_ref[0, h]
        sa = jnp.sum(s * (-kk[:, hs]), axis=1, keepdims=True)
        s_new = s * w[:, hs] + sa * b[:, hs] + col(v[:, hs]) * k2[:, hs]
        sn_ref[0, h] = s_new
        o_col = jnp.sum(s_new * r[:, hs], axis=1, keepdims=True)
        o_heads.append(jnp.sum(jnp.where(eye, o_col, 0.0), axis=0, keepdims=True))
    o = jnp.concatenate(o_heads, axis=1)
    o_ref[0] = _rwkv_post(o, r, k2, v, g, rk_ref[...], lng_ref[...], lnb_ref[...], ones)


def _rwkv_sample(z_r, shift, wkv, prm):
    B = z_r.shape[0]
    row = lambda x: x.reshape(1, -1)
    full = lambda a: pl.BlockSpec(a.shape, lambda b: (0,) * a.ndim)
    params = [row(prm["rwkv_mu"]), row(prm["rwkv_w0"]), prm["rwkv_w_up"], row(prm["rwkv_a0"]),
              prm["rwkv_a_up"], prm["rwkv_g_up"], row(prm["rwkv_k_k"]), row(prm["rwkv_k_a"]),
              row(prm["rwkv_r_k"]), row(prm["rwkv_ln_g"]), row(prm["rwkv_ln_b"])]
    o, s_new = pl.pallas_call(
        _rwkv_sample_kernel,
        out_shape=(jax.ShapeDtypeStruct((B, 1, R_WIDTH), F32),
                   jax.ShapeDtypeStruct(wkv.shape, F32)),
        grid=(B,),
        in_specs=[pl.BlockSpec((1, 1, R_COLS), lambda b: (b, 0, 0)),
                  pl.BlockSpec((1, 1, R_COLS), lambda b: (b, 0, 0)),
                  pl.BlockSpec((1,) + wkv.shape[1:], lambda b: (b, 0, 0, 0))] + [full(a) for a in params],
        out_specs=(pl.BlockSpec((1, 1, R_WIDTH), lambda b: (b, 0, 0)),
                   pl.BlockSpec((1,) + wkv.shape[1:], lambda b: (b, 0, 0, 0))),
        compiler_params=_cparams(("parallel",)),
        name="rwkv_sample",
    )(z_r.reshape(B, 1, R_COLS), shift.reshape(B, 1, R_COLS), wkv, *params)
    return o.reshape(B, R_WIDTH), s_new


def _conv_prompt_kernel(z_ref, dw_ref, dwb_ref, lng_ref, lnb_ref, wout_ref, y_ref, nb_ref, u_scr, sh_scr,
                        acc_scr, *, t):
    i = pl.program_id(1)
    pad = 32

    @pl.when(i == 0)
    def _():
        u_scr[0:pad, :] = jnp.zeros((pad, CONV_CH), F32)
        u_scr[pad + t:, :] = jnp.zeros((SUBLANES, CONV_CH), F32)

    z = z_ref[0]
    u_scr[pad:pad + t, :] = z[:, :CONV_CH] * _sigmoid(z[:, CONV_CH:])
    span = t + pad
    for s in range(SUBLANES):
        sh_scr[s, 0:span, :] = u_scr[s:s + span, :]
    off = pad - (CONV_K - 1)

    def rows_block(bi, carry):
        r0 = pl.multiple_of(bi * CONV_ROWS, CONV_ROWS)
        acc = jnp.zeros((CONV_ROWS, CONV_CH), F32) + dwb_ref[...]
        for j in range(CONV_K):
            a, s = divmod(off + j, SUBLANES)
            acc = acc + dw_ref[j:j + 1, :] * sh_scr[s, pl.ds(r0 + a * SUBLANES, CONV_ROWS), :]
        acc_scr[pl.ds(r0, CONV_ROWS), :] = acc
        return carry

    lax.fori_loop(0, t // CONV_ROWS, rows_block, 0)
    tail = u_scr[t:t + pad, :]
    nb_ref[0] = tail
    u_scr[0:pad, :] = tail
    c = _layer_norm(acc_scr[...], lng_ref[...], lnb_ref[...])
    c = c * _sigmoid(c)
    y_ref[0] = _dot(c, wout_ref[...]).astype(y_ref.dtype)


def _conv_prompt(z_c, prm, *, t=512):
    B, L, _ = z_c.shape
    t = min(t, L)
    assert L % t == 0 and t >= 32 and t % CONV_ROWS == 0
    row = lambda x: x.reshape(1, -1)
    full = lambda a: pl.BlockSpec(a.shape, lambda b, i: (0,) * a.ndim)
    dw = jnp.pad(prm["conv_dw"], ((0, 32 - CONV_K), (0, 0)))
    params = [dw, row(prm["conv_dw_b"]), row(prm["conv_ln_g"]), row(prm["conv_ln_b"]), prm["w_conv_out_bf"]]
    d_out = prm["w_conv_out_bf"].shape[1]
    y, nb = pl.pallas_call(
        functools.partial(_conv_prompt_kernel, t=t),
        out_shape=(jax.ShapeDtypeStruct((B, L, d_out), BF16), jax.ShapeDtypeStruct((B, 32, CONV_CH), F32)),
        grid=(B, L // t),
        in_specs=[pl.BlockSpec((1, t, 2 * CONV_CH), lambda b, i: (b, i, 0))] + [full(a) for a in params],
        out_specs=(pl.BlockSpec((1, t, d_out), lambda b, i: (b, i, 0)),
                   pl.BlockSpec((1, 32, CONV_CH), lambda b, i: (b, 0, 0))),
        scratch_shapes=[pltpu.VMEM((32 + t + SUBLANES, CONV_CH), F32),
                        pltpu.VMEM((SUBLANES, 32 + t, CONV_CH), F32),
                        pltpu.VMEM((t, CONV_CH), F32)],
        compiler_params=_cparams(("parallel", "arbitrary")),
        name="conv_prompt",
    )(z_c, *params)
    return y, nb[:, 32 - (CONV_K - 1):]


def _conv_sample_kernel(z_ref, buf_ref, dw_ref, dwb_ref, lng_ref, lnb_ref, c_ref, u_ref):
    z = z_ref[...]
    u = z[:, :CONV_CH] * _sigmoid(z[:, CONV_CH:])
    u_ref[...] = u
    acc = dwb_ref[...] + dw_ref[CONV_K - 1:CONV_K, :] * u
    for j in range(CONV_K - 1):
        acc = acc + dw_ref[j:j + 1, :] * buf_ref[:, j, :]
    c = _layer_norm(acc, lng_ref[...], lnb_ref[...])
    c_ref[...] = c * _sigmoid(c)


def _conv_sample(z_c, buf, prm):
    B = z_c.shape[0]
    row = lambda x: x.reshape(1, -1)
    args = [z_c, buf, prm["conv_dw"], row(prm["conv_dw_b"]), row(prm["conv_ln_g"]), row(prm["conv_ln_b"])]
    return pl.pallas_call(
        _conv_sample_kernel,
        out_shape=(jax.ShapeDtypeStruct((B, CONV_CH), F32), jax.ShapeDtypeStruct((B, CONV_CH), F32)),
        compiler_params=pltpu.CompilerParams(vmem_limit_bytes=VMEM_LIMIT),
        name="conv_sample",
    )(*args)


def _attn_group(q_ref, k_ref, v_ref, o_ref, m_scr, l_scr, acc_scr, *, seq, dil, nk, first, last):
    nb = (seq // dil) // Q_BLOCK
    qi = _iota((Q_BLOCK, 2 * Q_BLOCK), 0)
    kj = _iota((Q_BLOCK, 2 * Q_BLOCK), 1)
    dist = qi + Q_BLOCK - kj
    in_win = (dist >= 0) & (dist <= nk)
    is_cur = kj >= Q_BLOCK
    lane_lo = _iota((Q_BLOCK, LANES), 1) < HEAD_DIM

    def rows(start):
        return pl.ds(start, Q_BLOCK) if dil == 1 else pl.ds(start, Q_BLOCK, stride=dil)

    def body(it, carry):
        blocks = []
        for u in range(ATTN_UNROLL):
            blk = it * ATTN_UNROLL + u
            d = blk // nb
            n = blk - d * nb
            q0 = d + n * (Q_BLOCK * dil)
            p0 = jnp.maximum(q0 - Q_BLOCK * dil, d)
            blocks.append((q0, p0, in_win & (is_cur | (n > 0))))
        qb = [q_ref[0, rows(q0), :] * ATTN_SCALE for q0, _, _ in blocks]
        kb = [jnp.concatenate([k_ref[0, rows(p0), :], k_ref[0, rows(q0), :]], axis=0).astype(BF16)
              for q0, p0, _ in blocks]
        vb = [jnp.concatenate([v_ref[0, rows(p0), :], v_ref[0, rows(q0), :]], axis=0).astype(BF16)
              for q0, p0, _ in blocks]
        units = [(u, lo) for u in range(ATTN_UNROLL) for lo in (True, False)]
        s = [_dot_nt(jnp.where(lane_lo == lo, qb[u], 0.0), kb[u]) for u, lo in units]
        s = [jnp.where(blocks[u][2], x, NEG_BIG) for (u, _), x in zip(units, s)]
        m = [jnp.max(x, axis=1, keepdims=True) for x in s]
        p = [jnp.exp(x - mm) for x, mm in zip(s, m)]
        l = [jnp.sum(x, axis=1, keepdims=True) for x in p]
        o = [jnp.dot(x.astype(BF16), vb[u], preferred_element_type=F32) for (u, _), x in zip(units, p)]
        for u in range(ATTN_UNROLL):
            m_b = jnp.where(lane_lo, m[2 * u], m[2 * u + 1])
            l_b = jnp.where(lane_lo, l[2 * u], l[2 * u + 1])
            o_b = jnp.where(lane_lo, o[2 * u], o[2 * u + 1])
            r = rows(blocks[u][0])
            if not first:
                m_old = m_scr[r, :]
                m_new = jnp.maximum(m_old, m_b)
                a_old = jnp.exp(m_old - m_new)
                a_new = jnp.exp(m_b - m_new)
                l_b = l_scr[r, :] * a_old + l_b * a_new
                o_b = acc_scr[r, :] * a_old + o_b * a_new
                m_b = m_new
            if last:
                o_ref[0, r, :] = o_b / l_b
            else:
                m_scr[r, :] = m_b
                l_scr[r, :] = l_b
                acc_scr[r, :] = o_b
        return carry

    lax.fori_loop(0, seq // (Q_BLOCK * ATTN_UNROLL), body, 0)


def _attn_prompt_kernel(q_ref, k_ref, v_ref, o_ref, m_scr, l_scr, acc_scr, *, seq):
    g = pl.program_id(2)
    for gi, (win, dil) in enumerate(SWA_GROUPS):
        @pl.when(g == gi)
        def _(dil=dil, nk=win // dil, gi=gi):
            _attn_group(q_ref, k_ref, v_ref, o_ref, m_scr, l_scr, acc_scr, seq=seq, dil=dil, nk=nk,
                        first=gi == 0, last=gi == N_GROUPS - 1)


def _attn_prompt(z_qkv):
    B, S, _ = z_qkv.shape
    for win, dil in SWA_GROUPS:
        assert S % (dil * Q_BLOCK) == 0 and win // dil <= Q_BLOCK
    assert (S // Q_BLOCK) % ATTN_UNROLL == 0
    pairs = G_WIDTH // LANES
    blk = lambda base: pl.BlockSpec((1, S, LANES), lambda b, p, g: (b, 0, base + g * pairs + p))
    return pl.pallas_call(
        functools.partial(_attn_prompt_kernel, seq=S),
        out_shape=jax.ShapeDtypeStruct((B, S, G_WIDTH), F32),
        grid=(B, pairs, N_GROUPS),
        in_specs=[blk(0), blk(A_WIDTH // LANES), blk(2 * A_WIDTH // LANES)],
        out_specs=pl.BlockSpec((1, S, LANES), lambda b, p, g: (b, 0, p)),
        scratch_shapes=[pltpu.VMEM((S, LANES), F32)] * 3,
        compiler_params=_cparams(("parallel", "parallel", "arbitrary")),
        name="attn_prompt",
    )(z_qkv, z_qkv, z_qkv)


def _attn_sample_kernel(q_ref, kn_ref, vn_ref, ca_ref, cb_ref, cc_ref, o_ref, *, bs):
    caches = (ca_ref, cb_ref, cc_ref)
    hrow = _iota((SUBLANES, G_WIDTH), 0)
    hmask = hrow == (_iota((SUBLANES, G_WIDTH), 1) >> 6)
    outs = []
    for n in range(bs):
        m_run = l_run = acc = None
        for gi, (win, dil) in enumerate(SWA_GROUPS):
            gs = slice(gi * G_WIDTH, (gi + 1) * G_WIDTH)
            q = q_ref[0, n:n + 1, gs] * ATTN_SCALE
            qr = jnp.where(hmask, q, 0.0).astype(BF16)
            k_t = caches[gi][0, n, 0].reshape(G_WIDTH, win).astype(BF16)
            v_t = caches[gi][0, n, 1].reshape(G_WIDTH, win).astype(BF16)
            kn = kn_ref[0, n:n + 1, gs].astype(BF16).astype(F32)
            vn = vn_ref[0, n:n + 1, gs].astype(BF16).astype(F32)
            valid = (_iota((SUBLANES, win), 1) & (dil - 1)) == 0
            s = jnp.where(valid, jnp.dot(qr, k_t, preferred_element_type=F32), NEG_BIG)
            s_n = jnp.sum(qr.astype(F32) * kn, axis=1, keepdims=True)
            m = jnp.maximum(jnp.max(s, axis=1, keepdims=True), s_n)
            p = jnp.exp(s - m)
            p_n = jnp.exp(s_n - m)
            l = jnp.sum(p, axis=1, keepdims=True) + p_n
            o = _dot_nt(p, v_t) + p_n.astype(BF16).astype(F32) * vn
            if m_run is None:
                m_run, l_run, acc = m, l, o
            else:
                m_new = jnp.maximum(m_run, m)
                a_old = jnp.exp(m_run - m_new)
                a_new = jnp.exp(m - m_new)
                l_run = l_run * a_old + l * a_new
                acc = acc * a_old + o * a_new
                m_run = m_new
        outs.append(jnp.sum(jnp.where(hmask, acc / l_run, 0.0), axis=0, keepdims=True))
    o_ref[0] = jnp.concatenate(outs, axis=0)


def _attn_sample(q, k_new, v_new, caches_t, layer, *, bs=2):
    B = q.shape[0]
    assert B % bs == 0
    for (win, dil), c in zip(SWA_GROUPS, caches_t):
        assert c.shape[-1] == win and win % dil == 0 and dil & (dil - 1) == 0
    vec = pl.BlockSpec((1, bs, A_WIDTH), lambda i: (i, 0, 0))
    cache_spec = lambda c: pl.BlockSpec((1, bs) + c.shape[2:], lambda i: (layer, i, 0, 0, 0, 0))
    rows = lambda x: x.reshape(B // bs, bs, A_WIDTH)
    out = pl.pallas_call(
        functools.partial(_attn_sample_kernel, bs=bs),
        out_shape=jax.ShapeDtypeStruct((B // bs, bs, G_WIDTH), F32),
        grid=(B // bs,),
        in_specs=[vec, vec, vec] + [cache_spec(c) for c in caches_t],
        out_specs=pl.BlockSpec((1, bs, G_WIDTH), lambda i: (i, 0, 0)),
        compiler_params=_cparams(("parallel",)),
        name="attn_sample",
    )(rows(q), rows(k_new), rows(v_new), *caches_t)
    return out.reshape(B, G_WIDTH)


def _cache_shift_kernel(new_ref, ca_ref, cb_ref, cc_ref, oa_ref, ob_ref, oc_ref, *, bs):
    rows = 2 * G_WIDTH
    for gi, (c_ref, o_ref) in enumerate(((ca_ref, oa_ref), (cb_ref, ob_ref), (cc_ref, oc_ref))):
        win = c_ref.shape[-1]
        last = _iota((rows, win), 1) == win - 1
        for n in range(bs):
            x = c_ref[0, n].reshape(rows, win)
            col = new_ref[0, gi * rows:(gi + 1) * rows, n:n + 1]
            o_ref[0, n] = jnp.where(last, col, pltpu.roll(x, win - 1, 1)).reshape(o_ref.shape[2:])


def _cache_shift(caches_t, new_cols, *, bs=2):
    depth, B = caches_t[0].shape[:2]
    assert B % bs == 0
    nb = B // bs
    spec = lambda c: pl.BlockSpec((1, bs) + c.shape[2:], lambda l, i: (l, i, 0, 0, 0, 0))
    n_rows = new_cols.shape[1]
    new_b = new_cols.reshape(depth, n_rows, nb, bs).transpose(0, 2, 1, 3).reshape(depth * nb, n_rows, bs)
    return pl.pallas_call(
        functools.partial(_cache_shift_kernel, bs=bs),
        out_shape=tuple(jax.ShapeDtypeStruct(c.shape, c.dtype) for c in caches_t),
        grid=(depth, nb),
        in_specs=[pl.BlockSpec((1, n_rows, bs), lambda l, i: (l * nb + i, 0, 0))] + [spec(c) for c in caches_t],
        out_specs=tuple(spec(c) for c in caches_t),
        compiler_params=_cparams(("parallel", "parallel")),
        name="cache_shift",
    )(new_b, *caches_t)


def _merge_kernel(gt_ref, yr_ref, yc_ref, ao_ref, x_ref, wa_ref, wo_ref, g_ref, b_ref, h_ref, *, alpha):
    d = x_ref.shape[1]
    gate = lambda i: gt_ref[:, i * d:(i + 1) * d].astype(F32)
    y_a = _dot(ao_ref[...], wa_ref[...])
    merged = gate(0) * yr_ref[...].astype(F32) + gate(1) * yc_ref[...].astype(F32) + gate(2) * y_a
    h_ref[...] = _layer_norm(alpha * x_ref[...] + _dot(merged, wo_ref[...]), g_ref[...], b_ref[...])


def _merge(gates, y_r, y_c, a_o, x, prm, *, alpha, t=512):
    M, D = x.shape
    t = min(t, M)
    assert M % t == 0
    row = lambda a: a.reshape(1, -1)
    tile = lambda a: pl.BlockSpec((t, a.shape[1]), lambda i: (i, 0))
    full = lambda a: pl.BlockSpec(a.shape, lambda i: (0, 0))
    acts = [gates, y_r, y_c, a_o, x]
    params = [prm["w_attn_out_bf"], prm["w_o_bf"], row(prm["ln1_g"]), row(prm["ln1_b"])]
    return pl.pallas_call(
        functools.partial(_merge_kernel, alpha=alpha),
        out_shape=jax.ShapeDtypeStruct((M, D), F32),
        grid=(M // t,),
        in_specs=[tile(a) for a in acts] + [full(a) for a in params],
        out_specs=pl.BlockSpec((t, D), lambda i: (i, 0)),
        compiler_params=_cparams(("parallel",)),
        name="merge",
    )(*acts, *params)


def _ffn_kernel(*refs, alpha, t, tiles_per_seq, sample):
    if sample:
        (h_ref, pg_ref, pu_ref, wg_ref, wu_ref, dg_ref, du_ref, bg_ref, bu_ref, wo_ref, g_ref, b_ref,
         o_ref, ng_ref, nu_ref, acc_scr) = refs
    else:
        (h_ref, wg_ref, wu_ref, dg_ref, du_ref, bg_ref, bu_ref, wo_ref, g_ref, b_ref,
         o_ref, ng_ref, nu_ref, acc_scr, cg_scr, cu_scr) = refs
    i = pl.program_id(0)
    j = pl.program_id(1)
    nj = pl.num_programs(1)
    hb = h_ref[...].astype(BF16)

    def conv(u, dw_ref, bias_ref, p1, p2):
        return dw_ref[0:1, :] * p2 + dw_ref[1:2, :] * p1 + dw_ref[2:3, :] * u + bias_ref[...]

    def branch(w_ref, dw_ref, bias_ref, new_ref, prev_ref_or_scr):
        u = jnp.dot(hb, w_ref[...], preferred_element_type=F32)
        if sample:
            new_ref[...] = u
            p2, p1 = prev_ref_or_scr[0], prev_ref_or_scr[1]
        else:
            new_ref[0] = u[t - 2:t, :]
            row = _iota((SUBLANES, u.shape[1]), 0)
            fresh = (i % tiles_per_seq) == 0
            c1 = jnp.where(fresh, 0.0, prev_ref_or_scr[j, 7:8, :])
            c2 = jnp.where(fresh, 0.0, prev_ref_or_scr[j, 6:7, :])
            r1 = pltpu.roll(u, 1, 0)
            r2 = pltpu.roll(u, 2, 0)
            p1 = jnp.concatenate([jnp.where(row == 0, c1, r1[:SUBLANES]), r1[SUBLANES:]], axis=0)
            p2 = jnp.concatenate([jnp.where(row == 0, c2, jnp.where(row == 1, c1, r2[:SUBLANES])),
                                  r2[SUBLANES:]], axis=0)
            prev_ref_or_scr[j, :, :] = u[t - 8:t, :]
        return conv(u, dw_ref, bias_ref, p1, p2)

    if sample:
        y_g = branch(wg_ref, dg_ref, bg_ref, ng_ref, pg_ref)
        y_u = branch(wu_ref, du_ref, bu_ref, nu_ref, pu_ref)
    else:
        y_g = branch(wg_ref, dg_ref, bg_ref, ng_ref, cg_scr)
        y_u = branch(wu_ref, du_ref, bu_ref, nu_ref, cu_scr)
    act = (y_g * _sigmoid(y_g)) * y_u
    part = _dot(act, wo_ref[...])

    @pl.when(j == 0)
    def _():
        acc_scr[...] = part

    @pl.when(j > 0)
    def _():
        acc_scr[...] += part

    @pl.when(j == nj - 1)
    def _():
        o_ref[...] = _layer_norm(alpha * h_ref[...] + acc_scr[...], g_ref[...], b_ref[...])


def _ffn(h, prm, *, alpha, seq_len, prev=None, t=512, tn=1408):
    M, D = h.shape
    d_ff = prm["w_ffn_out_bf"].shape[0]
    sample = prev is not None
    t = M if sample else min(t, seq_len)
    tn = min(tn, d_ff)
    assert M % t == 0 and d_ff % tn == 0 and (sample or (seq_len % t == 0 and t >= 8))
    J = d_ff // tn
    row = lambda a: a.reshape(1, -1)
    w_in, dw, dwb = prm["w_ffn_in_bf"], prm["ffn_dw"], row(prm["ffn_dw_b"])
    col_g = lambda shape: pl.BlockSpec(shape, lambda i, j: (0, j))
    col_u = lambda shape: pl.BlockSpec(shape, lambda i, j: (0, J + j))
    const = lambda a: pl.BlockSpec(a.shape, lambda i, j: (0, 0))
    ins, specs = [h], [pl.BlockSpec((t, D), lambda i, j: (i, 0))]
    if sample:
        pstack = jnp.stack(prev)
        ins += [pstack, pstack]
        specs += [pl.BlockSpec((2, M, tn), lambda i, j: (0, 0, j)), pl.BlockSpec((2, M, tn), lambda i, j: (0, 0, J + j))]
    ins += [w_in, w_in, dw, dw, dwb, dwb, prm["w_ffn_out_bf"], row(prm["ln2_g"]), row(prm["ln2_b"])]
    specs += [col_g((D, tn)), col_u((D, tn)), col_g((FFN_K, tn)), col_u((FFN_K, tn)), col_g((1, tn)),
              col_u((1, tn)), pl.BlockSpec((tn, D), lambda i, j: (j, 0)), const(row(prm["ln2_g"])),
              const(row(prm["ln2_b"]))]
    if sample:
        new_shape = jax.ShapeDtypeStruct((M, d_ff), F32)
        new_spec = pl.BlockSpec((M, tn), lambda i, j: (0, j))
        scratch = [pltpu.VMEM((t, D), F32)]
        tiles_per_seq = 1
    else:
        tiles_per_seq = seq_len // t
        new_shape = jax.ShapeDtypeStruct((M // t, 2, d_ff), F32)
        new_spec = pl.BlockSpec((1, 2, tn), lambda i, j: (i, 0, j))
        scratch = [pltpu.VMEM((t, D), F32), pltpu.VMEM((J, 8, tn), F32), pltpu.VMEM((J, 8, tn), F32)]
    out, new_g, new_u = pl.pallas_call(
        functools.partial(_ffn_kernel, alpha=alpha, t=t, tiles_per_seq=tiles_per_seq, sample=sample),
        out_shape=(jax.ShapeDtypeStruct((M, D), F32), new_shape, new_shape),
        grid=(M // t, J),
        in_specs=specs,
        out_specs=(pl.BlockSpec((t, D), lambda i, j: (i, 0)), new_spec, new_spec),
        scratch_shapes=scratch,
        compiler_params=_cparams(("arbitrary", "arbitrary")),
        name="ffn_sample" if sample else "ffn_prompt",
    )(*ins)
    if not sample:
        new_g = new_g[tiles_per_seq - 1::tiles_per_seq]
        new_u = new_u[tiles_per_seq - 1::tiles_per_seq]
    return out, new_g, new_u


def _layer_weights(l, w_in, w_rwkv_out, w_conv_out, w_attn_out, w_o, w_ffn_in, w_ffn_out, small):
    prm = {k: v[l] for k, v in small.items()}
    wi = w_in[l]
    o_c = R_COLS
    o_q = o_c + 2 * CONV_CH
    o_g = o_q + 3 * A_WIDTH
    prm["w_in_r"] = wi[:, :o_c].astype(BF16)
    prm["w_in_c"] = wi[:, o_c:o_q].astype(BF16)
    prm["w_in_qkv"] = wi[:, o_q:o_g].astype(BF16)
    prm["w_in_g"] = wi[:, o_g:].astype(BF16)
    prm["w_rwkv_out_bf"] = w_rwkv_out[l].astype(BF16)
    prm["w_conv_out_bf"] = w_conv_out[l].astype(BF16)
    prm["w_attn_out_bf"] = w_attn_out[l].astype(BF16)
    prm["w_o_bf"] = w_o[l].astype(BF16)
    prm["w_ffn_in_bf"] = w_ffn_in[l].astype(BF16)
    prm["w_ffn_out_bf"] = w_ffn_out[l].astype(BF16)
    return prm


def _project(x2, prm, tm):
    z_r = _mm(x2, prm["w_in_r"], tm=tm, tn=R_COLS)
    z_c = _mm(x2, prm["w_in_c"], tm=tm, tn=2 * CONV_CH)
    z_qkv = _mm(x2, prm["w_in_qkv"], tm=tm, tn=3 * A_WIDTH // 2)
    gates = _mm(x2, prm["w_in_g"], tm=tm, tn=1024, out_dtype=BF16, gate_bias=prm["gate_b"].reshape(1, -1))
    return z_r, z_c, z_qkv, gates


def _kv_rows(z_qkv, gi):
    k = z_qkv[..., A_WIDTH + gi * G_WIDTH:A_WIDTH + (gi + 1) * G_WIDTH]
    v = z_qkv[..., 2 * A_WIDTH + gi * G_WIDTH:2 * A_WIDTH + (gi + 1) * G_WIDTH]
    kv = jnp.stack([k, v], axis=-2)
    return kv.reshape(kv.shape[:-1] + (G_HEADS, HEAD_DIM))


def _prompt_layer(x, prm, alpha):
    B, L, D = x.shape
    x2 = x.reshape(B * L, D)
    z_r, z_c, z_qkv, z_g = _project(x2, prm, tm=1024)
    z_r = z_r.reshape(B, L, -1)
    z_qkv = z_qkv.reshape(B, L, -1)
    y_r, new_wkv = _rwkv_prompt(z_r, prm)
    y_c, new_conv = _conv_prompt(z_c.reshape(B, L, -1), prm)
    a_o = _attn_prompt(z_qkv)
    h = _merge(z_g, y_r.reshape(B * L, D), y_c.reshape(B * L, D), a_o.reshape(B * L, -1), x2, prm, alpha=alpha)
    out, nf_g, nf_u = _ffn(h, prm, alpha=alpha, seq_len=L)
    new_kv = [_kv_rows(z_qkv[:, L - min(win, L):], gi) for gi, (win, _) in enumerate(SWA_GROUPS)]
    return (out.reshape(B, L, D), z_r[:, L - 1:], new_wkv, new_conv, new_kv,
            jnp.concatenate([nf_g, nf_u], axis=-1))


def _sample_layer(x, prm, alpha, shift, wkv, conv_buf, ffn_buf, caches_t, layer):
    B, L, D = x.shape
    assert L == 1
    x2 = x.reshape(B, D)
    z_r, z_c, z_qkv, z_g = _project(x2, prm, tm=B)
    o_r, new_wkv = _rwkv_sample(z_r, shift.reshape(B, R_COLS), wkv, prm)
    y_r = _mm(o_r, prm["w_rwkv_out_bf"], tm=B, tn=1024, out_dtype=BF16)
    c, u = _conv_sample(z_c, conv_buf, prm)
    y_c = _mm(c, prm["w_conv_out_bf"], tm=B, tn=1024, out_dtype=BF16)
    a_o = _attn_sample(z_qkv[:, :A_WIDTH], z_qkv[:, A_WIDTH:2 * A_WIDTH], z_qkv[:, 2 * A_WIDTH:], caches_t, layer)
    h = _merge(z_g, y_r, y_c, a_o, x2, prm, alpha=alpha)
    out, u_g, u_u = _ffn(h, prm, alpha=alpha, seq_len=1, prev=(ffn_buf[:, 0], ffn_buf[:, 1]))
    kv_new = []
    for gi in range(N_GROUPS):
        kv_new += [z_qkv[:, A_WIDTH + gi * G_WIDTH:A_WIDTH + (gi + 1) * G_WIDTH],
                   z_qkv[:, 2 * A_WIDTH + gi * G_WIDTH:2 * A_WIDTH + (gi + 1) * G_WIDTH]]
    return (out.reshape(B, 1, D), z_r.reshape(B, 1, R_COLS), new_wkv, u, jnp.concatenate(kv_new, axis=-1),
            jnp.concatenate([u_g, u_u], axis=-1))


def _append_row(buf, rows):
    return jnp.concatenate([buf[:, :, 1:], rows[:, :, None]], axis=2)


def kernel(x_prompt, x_sample, state_shift, state_wkv, state_conv, cache_swa_a, cache_swa_b, cache_swa_c, state_ffn, w_in, rwkv_mu, rwkv_w0, rwkv_w_up, rwkv_a0, rwkv_a_up, rwkv_g_up, rwkv_k_k, rwkv_k_a, rwkv_r_k, rwkv_ln_g, rwkv_ln_b, w_rwkv_out, conv_dw, conv_dw_b, conv_ln_g, conv_ln_b, w_conv_out, w_attn_out, gate_b, w_o, ln1_g, ln1_b, w_ffn_in, ffn_dw, ffn_dw_b, w_ffn_out, ln2_g, ln2_b):
    depth = w_in.shape[0]
    alpha = (2 * depth) ** 0.25
    small = dict(rwkv_mu=rwkv_mu, rwkv_w0=rwkv_w0, rwkv_w_up=rwkv_w_up, rwkv_a0=rwkv_a0, rwkv_a_up=rwkv_a_up,
                 rwkv_g_up=rwkv_g_up, rwkv_k_k=rwkv_k_k, rwkv_k_a=rwkv_k_a,
                 rwkv_r_k=rwkv_r_k.reshape(depth, -1), rwkv_ln_g=rwkv_ln_g, rwkv_ln_b=rwkv_ln_b,
                 conv_dw=conv_dw, conv_dw_b=conv_dw_b, conv_ln_g=conv_ln_g, conv_ln_b=conv_ln_b,
                 gate_b=gate_b.reshape(depth, -1), ln1_g=ln1_g, ln1_b=ln1_b, ffn_dw=ffn_dw, ffn_dw_b=ffn_dw_b,
                 ln2_g=ln2_g, ln2_b=ln2_b)
    caches = (cache_swa_a, cache_swa_b, cache_swa_c)
    caches_t = [jnp.transpose(c, (0, 1, 3, 4, 5, 2)) for c in caches]
    hp, hs = x_prompt, x_sample
    outs_p, outs_s = [], []
    for l in range(depth):
        prm = _layer_weights(l, w_in, w_rwkv_out, w_conv_out, w_attn_out, w_o, w_ffn_in, w_ffn_out, small)
        res_p = _prompt_layer(hp, prm, alpha)
        hp = res_p[0]
        outs_p.append(res_p[1:])
        res_s = _sample_layer(hs, prm, alpha, state_shift[l], state_wkv[l], state_conv[l], state_ffn[l],
                              caches_t, l)
        hs = res_s[0]
        outs_s.append(res_s[1:])
    stk = lambda outs, f: jnp.stack([f(o) for o in outs])
    res = [hp, hs]
    res += [stk(outs_p, lambda o: o[0]), stk(outs_s, lambda o: o[0])]
    res += [stk(outs_p, lambda o: o[1]), stk(outs_s, lambda o: o[1])]
    res += [stk(outs_p, lambda o: o[2]), _append_row(state_conv, stk(outs_s, lambda o: o[2]))]
    new_cols = jnp.transpose(stk(outs_s, lambda o: o[3]), (0, 2, 1))
    shifted = _cache_shift(caches_t, new_cols)
    for gi in range(N_GROUPS):
        res += [stk(outs_p, lambda o: o[3][gi]), jnp.transpose(shifted[gi], (0, 1, 5, 2, 3, 4))]
    res += [stk(outs_p, lambda o: o[4]), _append_row(state_ffn, stk(outs_s, lambda o: o[4]))]
    return tuple(res)
```

```python
import functools

import jax
import jax.numpy as jnp
from jax import lax
from jax.experimental import pallas as pl
from jax.experimental.pallas import tpu as pltpu

F32 = jnp.float32
BF16 = jnp.bfloat16

HEAD_DIM = 64
R_HEADS = 8
R_WIDTH = R_HEADS * HEAD_DIM
DECAY_LORA = 64
ICLR_LORA = 64
GATE_LORA = 128
R_COLS = 3 * R_WIDTH + DECAY_LORA + ICLR_LORA + GATE_LORA
LNX_EPS = 64e-5
CONV_CH = 512
CONV_K = 31
SWA_GROUPS = ((128, 1), (512, 4), (2048, 16))
N_GROUPS = len(SWA_GROUPS)
G_HEADS = 4
G_WIDTH = G_HEADS * HEAD_DIM
A_WIDTH = N_GROUPS * G_WIDTH
Q_BLOCK = 128
ATTN_SCALE = HEAD_DIM ** -0.5
N_BRANCH = 3
FFN_K = 3
LN_EPS = 1e-5
NEG_BIG = -1e30

Z_QKV0 = R_COLS
Z_C0 = Z_QKV0 + 3 * A_WIDTH
Z_WIDTH = Z_C0 + 2 * CONV_CH

LANES = 128
CHUNK = 64
SUBLANES = 8
CONV_ROWS = 32
SEG_BLOCK = 256
ATTN_UNROLL = 8
VMEM_LIMIT = 56 * 1024 * 1024


def _cparams(sem):
    return pltpu.CompilerParams(dimension_semantics=sem, vmem_limit_bytes=VMEM_LIMIT)


def _dot(a, b):
    return jnp.dot(a.astype(BF16), b.astype(BF16), preferred_element_type=F32)


def _dot_nt(a, b):
    return lax.dot_general(a.astype(BF16), b.astype(BF16), (((1,), (1,)), ((), ())),
                           preferred_element_type=F32)


def _dot_tn(a, b):
    return lax.dot_general(a.astype(BF16), b.astype(BF16), (((0,), (0,)), ((), ())),
                           preferred_element_type=F32)


def _split3(x):
    h1 = x.astype(BF16)
    r1 = x - h1.astype(F32)
    h2 = r1.astype(BF16)
    h3 = (r1 - h2.astype(F32)).astype(BF16)
    return h1, h2, h3


def _sigmoid(x):
    return 1.0 / (1.0 + jnp.exp(-x))


def _layer_norm(x, g, b):
    mu = jnp.mean(x, axis=-1, keepdims=True)
    xc = x - mu
    var = jnp.mean(xc * xc, axis=-1, keepdims=True)
    return xc * lax.rsqrt(var + LN_EPS) * g + b


def _iota(shape, dim):
    return lax.broadcasted_iota(jnp.int32, shape, dim)


def _head_ones():
    r = _iota((SEG_BLOCK, SEG_BLOCK), 0)
    c = _iota((SEG_BLOCK, SEG_BLOCK), 1)
    return jnp.where((r >> 6) == (c >> 6), 1.0, 0.0).astype(BF16)


def _segsum(x, ones):
    outs = [_dot(x[:, p:p + SEG_BLOCK], ones) for p in range(0, x.shape[1], SEG_BLOCK)]
    return outs[0] if len(outs) == 1 else jnp.concatenate(outs, axis=1)


def _mm_kernel(x_ref, w_ref, o_ref):
    o_ref[...] = jnp.dot(x_ref[...].astype(BF16), w_ref[...],
                         preferred_element_type=F32).astype(o_ref.dtype)


def _mm(x, w, *, tm, tn, out_dtype=F32):
    M, K = x.shape
    N = w.shape[1]
    tm = min(tm, M)
    tn = min(tn, N)
    assert M % tm == 0 and N % tn == 0
    return pl.pallas_call(
        _mm_kernel,
        out_shape=jax.ShapeDtypeStruct((M, N), out_dtype),
        grid=(M // tm, N // tn),
        in_specs=[pl.BlockSpec((tm, K), lambda i, j: (i, 0)),
                  pl.BlockSpec((K, tn), lambda i, j: (0, j))],
        out_specs=pl.BlockSpec((tm, tn), lambda i, j: (i, j)),
        compiler_params=_cparams(("parallel", "parallel")),
        name="mm",
    )(x, w)


def _rwkv_prep(zr, prevs, mu, w0, w_up, a0, a_up, g_up, k_k, k_a, ones):
    zs = zr + (prevs - zr) * mu
    r = zs[:, 0:R_WIDTH]
    k = zs[:, R_WIDTH:2 * R_WIDTH]
    v = zs[:, 2 * R_WIDTH:3 * R_WIDTH]
    o1 = 3 * R_WIDTH
    wd = zs[:, o1:o1 + DECAY_LORA]
    ad = zs[:, o1 + DECAY_LORA:o1 + DECAY_LORA + ICLR_LORA]
    gd = zs[:, o1 + DECAY_LORA + ICLR_LORA:R_COLS]
    nx = -(w0 + _dot(jnp.tanh(wd), w_up))
    softplus = jnp.maximum(nx, 0.0) + jnp.log(1.0 + jnp.exp(-jnp.abs(nx)))
    logdec = -jnp.exp(-softplus - 0.5)
    a = _sigmoid(a0 + _dot(ad, a_up))
    g = _dot(_sigmoid(gd), g_up)
    kk = k * k_k
    kk = kk / jnp.maximum(jnp.sqrt(_segsum(kk * kk, ones)), 1e-12)
    k2 = k * (1.0 + (a - 1.0) * k_a)
    return r, k2, v, kk, a, g, logdec


def _rwkv_post(o, r, k2, v, g, r_k, ln_g, ln_b, ones):
    inv = 1.0 / HEAD_DIM
    m = _segsum(o, ones) * inv
    oc = o - m
    var = _segsum(oc * oc, ones) * inv
    on = oc * lax.rsqrt(var + LNX_EPS) * ln_g + ln_b
    bonus = _segsum(r * k2 * r_k, ones) * v
    return (on + bonus) * g


def _chunk_cumsum(logdec, tri_bd):
    h1, h2, h3 = _split3(logdec)
    d = lambda x: jnp.dot(tri_bd, x, preferred_element_type=F32)
    return d(h1) + d(h2) + d(h3)


def _rwkv_chunk_tables(logdec, cum_all, r, k2, v, kk, a, c):
    sl = slice(c * CHUNK, (c + 1) * CHUNK)
    cum = cum_all[sl]
    ld = logdec[sl]
    cum_end = cum[CHUNK - 1:CHUNK, :]
    p_in = jnp.exp(cum)
    p_ex = jnp.exp(cum - ld)
    p_inv = jnp.exp(-cum)
    p_tail = jnp.exp(cum_end - cum)
    kka = kk[sl] * a[sl]
    return dict(rt=r[sl] * p_in, at=-kk[sl] * p_ex, bt=kka * p_inv, kt=k2[sl] * p_inv,
                bh=kka * p_tail, kh=k2[sl] * p_tail, v=v[sl], pc=jnp.exp(cum_end))


def _rwkv_units(tables, n_pairs, masks):
    strict, incl, eye, lane_lo, row_lo, lane_lo_n = masks
    n = 2 * CHUNK
    units = [(tb, slice(p * LANES, (p + 1) * LANES)) for tb in tables for p in range(n_pairs)]
    swap = lambda x: pltpu.roll(x, HEAD_DIM, 1)
    masked = lambda x: jnp.concatenate([jnp.where(lane_lo, x, 0.0), jnp.where(lane_lo, 0.0, x)], axis=0)
    local_lo = lambda x: jnp.concatenate([x, swap(x)], axis=0)
    local_hi = lambda x: jnp.concatenate([swap(x), x], axis=0)
    twice = lambda x: jnp.concatenate([x, x], axis=0)

    def spread(x):
        return jnp.where(row_lo, jnp.where(lane_lo_n, x, 0.0), jnp.where(lane_lo_n, 0.0, swap(x)))

    aa, a_lo, v_hi, r_m, bk_m = [], [], [], [], []
    for tb, ls in units:
        r_m.append(masked(tb["rt"][:, ls]))
        aa.append(_dot_nt(jnp.concatenate([masked(tb["at"][:, ls]), r_m[-1]], axis=0),
                          jnp.concatenate([twice(tb["bt"][:, ls]), twice(tb["kt"][:, ls])], axis=0)))
        a_lo.append(local_lo(tb["at"][:, ls]))
        v_hi.append(local_hi(tb["v"][:, ls]))
        bk_m.append(jnp.concatenate([masked(tb["bh"][:, ls]), masked(tb["kh"][:, ls])], axis=0))
    lj = [jnp.where(strict, a[:n, :n], 0.0) for a in aa]
    x = [jnp.where(lane_lo_n, al, _dot(jnp.where(strict, a[:n, n:], 0.0), vh))
         for al, vh, a in zip(a_lo, v_hi, aa)]
    steps = CHUNK.bit_length() - 1
    for j in range(steps):
        if j < steps - 1:
            prod = [_dot(l, jnp.concatenate([l, xx], axis=1)) for l, xx in zip(lj, x)]
            lj = [pr[:, :n] for pr in prod]
            x = [xx + pr[:, n:] for xx, pr in zip(x, prod)]
        else:
            x = [xx + _dot(l, xx) for l, xx in zip(lj, x)]
    rhs = [jnp.concatenate([xx, jnp.where(lane_lo_n, 0.0, vh)], axis=0) for xx, vh in zip(x, v_hi)]
    qo = [_dot(jnp.concatenate([jnp.where(incl, a[n:, :n], 0.0), jnp.where(incl, a[n:, n:], 0.0)], axis=1), rh)
          for a, rh in zip(aa, rhs)]
    gh = [_dot_tn(bk, rh) for bk, rh in zip(bk_m, rhs)]
    return [(rm + spread(q), q, spread(g) + jnp.where(eye, tb["pc"][:, ls], 0.0), g)
            for (tb, ls), rm, q, g in zip(units, r_m, qo, gh)]


def _rwkv_masks():
    n = 2 * CHUNK
    rr = _iota((n, n), 0)
    cc = _iota((n, n), 1)
    same = (rr >> 6) == (cc >> 6)
    strict = same & ((cc & 63) < (rr & 63))
    incl = same & ((cc & 63) <= (rr & 63))
    eye = rr == cc
    lane_lo = _iota((CHUNK, LANES), 1) < HEAD_DIM
    return strict, incl, eye, lane_lo, rr < CHUNK, cc < HEAD_DIM


def _rwkv_prompt_kernel(z_ref, mu_ref, w0_ref, wup_ref, a0_ref, aup_ref, gup_ref, kk_ref, ka_ref,
                        rk_ref, lng_ref, lnb_ref, wout_ref, y_ref, st_ref, prev_scr, st_scr, *, ct):
    i = pl.program_id(1)

    @pl.when(i == 0)
    def _():
        prev_scr[...] = jnp.zeros_like(prev_scr)
        st_scr[...] = jnp.zeros_like(st_scr)

    zr = z_ref[0]
    row = _iota(zr.shape, 0)
    prevs = jnp.where(row == 0, prev_scr[0:1, :], pltpu.roll(zr, 1, 0))
    prev_scr[0:1, :] = zr[ct - 1:ct, :]
    ones = _head_ones()
    r, k2, v, kk, a, g, logdec = _rwkv_prep(
        zr, prevs, mu_ref[...], w0_ref[...], wup_ref[...], a0_ref[...], aup_ref[...], gup_ref[...],
        kk_ref[...], ka_ref[...], ones)

    rr = _iota((ct, ct), 0)
    cc = _iota((ct, ct), 1)
    tri_bd = jnp.where(((rr >> 6) == (cc >> 6)) & (cc <= rr), 1.0, 0.0).astype(BF16)
    masks = _rwkv_masks()
    n_pairs = R_WIDTH // LANES
    cum_all = _chunk_cumsum(logdec, tri_bd)
    n_chunks = ct // CHUNK
    tables = [_rwkv_chunk_tables(logdec, cum_all, r, k2, v, kk, a, c) for c in range(n_chunks)]
    units = _rwkv_units(tables, n_pairs, masks)
    states = [st_scr[p] for p in range(n_pairs)]
    lane_lo, lane_lo_n = masks[3], masks[5]
    o_rows = []
    for c in range(n_chunks):
        cur = units[c * n_pairs:(c + 1) * n_pairs]
        ostk = [_dot(qp, st) + op for (qp, op, _, _), st in zip(cur, states)]
        states = [jnp.where(lane_lo_n, 0.0, _dot(gm, st) + hm) for (_, _, gm, hm), st in zip(cur, states)]
        o_rows.append(jnp.concatenate(
            [jnp.where(lane_lo, pltpu.roll(o[:CHUNK], HEAD_DIM, 1), o[CHUNK:]) for o in ostk], axis=1))
    for p in range(n_pairs):
        st_scr[p] = states[p]
    o = o_rows[0] if len(o_rows) == 1 else jnp.concatenate(o_rows, axis=0)
    out = _rwkv_post(o, r, k2, v, g, rk_ref[...], lng_ref[...], lnb_ref[...], ones)
    y_ref[0] = _dot(out, wout_ref[...]).astype(y_ref.dtype)
    st_ref[0] = st_scr[...]


def _rwkv_prompt(z_r, prm, *, ct=256):
    B, L, _ = z_r.shape
    assert L % ct == 0 and ct % CHUNK == 0
    n_pairs = R_WIDTH // LANES
    row = lambda x: x.reshape(1, -1)
    full = lambda a: pl.BlockSpec(a.shape, lambda b, i: (0,) * a.ndim)
    params = [row(prm["rwkv_mu"]), row(prm["rwkv_w0"]), prm["rwkv_w_up"], row(prm["rwkv_a0"]),
              prm["rwkv_a_up"], prm["rwkv_g_up"], row(prm["rwkv_k_k"]), row(prm["rwkv_k_a"]),
              row(prm["rwkv_r_k"]), row(prm["rwkv_ln_g"]), row(prm["rwkv_ln_b"]), prm["w_rwkv_out_bf"]]
    y, st = pl.pallas_call(
        functools.partial(_rwkv_prompt_kernel, ct=ct),
        out_shape=(jax.ShapeDtypeStruct((B, L, prm["w_rwkv_out_bf"].shape[1]), BF16),
                   jax.ShapeDtypeStruct((B, n_pairs, LANES, LANES), F32)),
        grid=(B, L // ct),
        in_specs=[pl.BlockSpec((1, ct, R_COLS), lambda b, i: (b, i, 0))] + [full(a) for a in params],
        out_specs=(pl.BlockSpec((1, ct, prm["w_rwkv_out_bf"].shape[1]), lambda b, i: (b, i, 0)),
                   pl.BlockSpec((1, n_pairs, LANES, LANES), lambda b, i: (b, 0, 0, 0))),
        scratch_shapes=[pltpu.VMEM((8, R_COLS), F32), pltpu.VMEM((n_pairs, LANES, LANES), F32)],
        compiler_params=_cparams(("parallel", "arbitrary")),
        name="rwkv_prompt",
    )(z_r, *params)
    st = st[..., HEAD_DIM:].reshape(B, R_HEADS, HEAD_DIM, HEAD_DIM)
    return y, jnp.swapaxes(st, -1, -2)


def _rwkv_sample_kernel(z_ref, sh_ref, s_ref, mu_ref, w0_ref, wup_ref, a0_ref, aup_ref, gup_ref,
                        kk_ref, ka_ref, rk_ref, lng_ref, lnb_ref, o_ref, sn_ref):
    ones = _head_ones()
    zr = z_ref[0]
    r, k2, v, kk, a, g, logdec = _rwkv_prep(
        zr, sh_ref[0], mu_ref[...], w0_ref[...], wup_ref[...], a0_ref[...], aup_ref[...],
        gup_ref[...], kk_ref[...], ka_ref[...], ones)
    w = jnp.exp(logdec)
    b = kk * a
    eye = _iota((HEAD_DIM, HEAD_DIM), 0) == _iota((HEAD_DIM, HEAD_DIM), 1)
    col = lambda x: jnp.sum(jnp.where(eye, x, 0.0), axis=1, keepdims=True)
    o_heads = []
    for h in range(R_HEADS):
        hs = slice(h * HEAD_DIM, (h + 1) * HEAD_DIM)
        s = s_ref[0, h]
        sa = jnp.sum(s * (-kk[:, hs]), axis=1, keepdims=True)
        s_new = s * w[:, hs] + sa * b[:, hs] + col(v[:, hs]) * k2[:, hs]
        sn_ref[0, h] = s_new
        o_col = jnp.sum(s_new * r[:, hs], axis=1, keepdims=True)
        o_heads.append(jnp.sum(jnp.where(eye, o_col, 0.0), axis=0, keepdims=True))
    o = jnp.concatenate(o_heads, axis=1)
    o_ref[0] = _rwkv_post(o, r, k2, v, g, rk_ref[...], lng_ref[...], lnb_ref[...], ones)


def _rwkv_sample(z_r, shift, wkv, prm):
    B = z_r.shape[0]
    row = lambda x: x.reshape(1, -1)
    full = lambda a: pl.BlockSpec(a.shape, lambda b: (0,) * a.ndim)
    params = [row(prm["rwkv_mu"]), row(prm["rwkv_w0"]), prm["rwkv_w_up"], row(prm["rwkv_a0"]),
              prm["rwkv_a_up"], prm["rwkv_g_up"], row(prm["rwkv_k_k"]), row(prm["rwkv_k_a"]),
              row(prm["rwkv_r_k"]), row(prm["rwkv_ln_g"]), row(prm["rwkv_ln_b"])]
    o, s_new = pl.pallas_call(
        _rwkv_sample_kernel,
        out_shape=(jax.ShapeDtypeStruct((B, 1, R_WIDTH), F32),
                   jax.ShapeDtypeStruct(wkv.shape, F32)),
        grid=(B,),
        in_specs=[pl.BlockSpec((1, 1, R_COLS), lambda b: (b, 0, 0)),
                  pl.BlockSpec((1, 1, R_COLS), lambda b: (b, 0, 0)),
                  pl.BlockSpec((1,) + wkv.shape[1:], lambda b: (b, 0, 0, 0))] + [full(a) for a in params],
        out_specs=(pl.BlockSpec((1, 1, R_WIDTH), lambda b: (b, 0, 0)),
                   pl.BlockSpec((1,) + wkv.shape[1:], lambda b: (b, 0, 0, 0))),
        compiler_params=_cparams(("parallel",)),
        name="rwkv_sample",
    )(z_r.reshape(B, 1, R_COLS), shift.reshape(B, 1, R_COLS), wkv, *params)
    return o.reshape(B, R_WIDTH), s_new


def _conv_prompt_kernel(z_ref, dw_ref, dwb_ref, lng_ref, lnb_ref, wout_ref, y_ref, nb_ref, u_scr, sh_scr,
                        acc_scr, *, t):
    i = pl.program_id(1)
    pad = 32

    @pl.when(i == 0)
    def _():
        u_scr[0:pad, :] = jnp.zeros((pad, CONV_CH), F32)
        u_scr[pad + t:, :] = jnp.zeros((SUBLANES, CONV_CH), F32)

    z = z_ref[0]
    u_scr[pad:pad + t, :] = z[:, :CONV_CH] * _sigmoid(z[:, CONV_CH:])
    span = t + pad
    for s in range(SUBLANES):
        sh_scr[s, 0:span, :] = u_scr[s:s + span, :]
    off = pad - (CONV_K - 1)

    def rows_block(bi, carry):
        r0 = pl.multiple_of(bi * CONV_ROWS, CONV_ROWS)
        acc = jnp.zeros((CONV_ROWS, CONV_CH), F32) + dwb_ref[...]
        for j in range(CONV_K):
            a, s = divmod(off + j, SUBLANES)
            acc = acc + dw_ref[j:j + 1, :] * sh_scr[s, pl.ds(r0 + a * SUBLANES, CONV_ROWS), :]
        acc_scr[pl.ds(r0, CONV_ROWS), :] = acc
        return carry

    lax.fori_loop(0, t // CONV_ROWS, rows_block, 0)
    tail = u_scr[t:t + pad, :]
    nb_ref[0] = tail
    u_scr[0:pad, :] = tail
    c = _layer_norm(acc_scr[...], lng_ref[...], lnb_ref[...])
    c = c * _sigmoid(c)
    y_ref[0] = _dot(c, wout_ref[...]).astype(y_ref.dtype)


def _conv_prompt(z, prm, *, t=512):
    B, L, _ = z.shape
    assert Z_C0 % (2 * CONV_CH) == 0
    t = min(t, L)
    assert L % t == 0 and t >= 32 and t % CONV_ROWS == 0
    row = lambda x: x.reshape(1, -1)
    full = lambda a: pl.BlockSpec(a.shape, lambda b, i: (0,) * a.ndim)
    dw = jnp.pad(prm["conv_dw"], ((0, 32 - CONV_K), (0, 0)))
    params = [dw, row(prm["conv_dw_b"]), row(prm["conv_ln_g"]), row(prm["conv_ln_b"]), prm["w_conv_out_bf"]]
    d_out = prm["w_conv_out_bf"].shape[1]
    y, nb = pl.pallas_call(
        functools.partial(_conv_prompt_kernel, t=t),
        out_shape=(jax.ShapeDtypeStruct((B, L, d_out), BF16), jax.ShapeDtypeStruct((B, 32, CONV_CH), F32)),
        grid=(B, L // t),
        in_specs=[pl.BlockSpec((1, t, 2 * CONV_CH), lambda b, i: (b, i, Z_C0 // (2 * CONV_CH)))]
                 + [full(a) for a in params],
        out_specs=(pl.BlockSpec((1, t, d_out), lambda b, i: (b, i, 0)),
                   pl.BlockSpec((1, 32, CONV_CH), lambda b, i: (b, 0, 0))),
        scratch_shapes=[pltpu.VMEM((32 + t + SUBLANES, CONV_CH), F32),
                        pltpu.VMEM((SUBLANES, 32 + t, CONV_CH), F32),
                        pltpu.VMEM((t, CONV_CH), F32)],
        compiler_params=_cparams(("parallel", "arbitrary")),
        name="conv_prompt",
    )(z, *params)
    return y, nb[:, 32 - (CONV_K - 1):]


def _conv_sample_kernel(z_ref, buf_ref, dw_ref, dwb_ref, lng_ref, lnb_ref, c_ref, u_ref):
    z = z_ref[...]
    u = z[:, :CONV_CH] * _sigmoid(z[:, CONV_CH:])
    u_ref[...] = u
    acc = dwb_ref[...] + dw_ref[CONV_K - 1:CONV_K, :] * u
    for j in range(CONV_K - 1):
        acc = acc + dw_ref[j:j + 1, :] * buf_ref[:, j, :]
    c = _layer_norm(acc, lng_ref[...], lnb_ref[...])
    c_ref[...] = c * _sigmoid(c)


def _conv_sample(z_c, buf, prm):
    B = z_c.shape[0]
    row = lambda x: x.reshape(1, -1)
    args = [z_c, buf, prm["conv_dw"], row(prm["conv_dw_b"]), row(prm["conv_ln_g"]), row(prm["conv_ln_b"])]
    return pl.pallas_call(
        _conv_sample_kernel,
        out_shape=(jax.ShapeDtypeStruct((B, CONV_CH), F32), jax.ShapeDtypeStruct((B, CONV_CH), F32)),
        compiler_params=pltpu.CompilerParams(vmem_limit_bytes=VMEM_LIMIT),
        name="conv_sample",
    )(*args)


def _attn_group(q_ref, k_ref, v_ref, o_ref, m_scr, l_scr, acc_scr, *, seq, dil, nk, first, last):
    nb = (seq // dil) // Q_BLOCK
    qi = _iota((Q_BLOCK, 2 * Q_BLOCK), 0)
    kj = _iota((Q_BLOCK, 2 * Q_BLOCK), 1)
    dist = qi + Q_BLOCK - kj
    in_win = (dist >= 0) & (dist <= nk)
    is_cur = kj >= Q_BLOCK
    lane_lo = _iota((Q_BLOCK, LANES), 1) < HEAD_DIM

    def rows(start):
        return pl.ds(start, Q_BLOCK) if dil == 1 else pl.ds(start, Q_BLOCK, stride=dil)

    def body(it, carry):
        blocks = []
        for u in range(ATTN_UNROLL):
            blk = it * ATTN_UNROLL + u
            d = blk // nb
            n = blk - d * nb
            q0 = d + n * (Q_BLOCK * dil)
            p0 = jnp.maximum(q0 - Q_BLOCK * dil, d)
            blocks.append((q0, p0, in_win & (is_cur | (n > 0))))
        qb = [q_ref[0, rows(q0), :] * ATTN_SCALE for q0, _, _ in blocks]
        kb = [jnp.concatenate([k_ref[0, rows(p0), :], k_ref[0, rows(q0), :]], axis=0).astype(BF16)
              for q0, p0, _ in blocks]
        vb = [jnp.concatenate([v_ref[0, rows(p0), :], v_ref[0, rows(q0), :]], axis=0).astype(BF16)
              for q0, p0, _ in blocks]
        units = [(u, lo) for u in range(ATTN_UNROLL) for lo in (True, False)]
        s = [_dot_nt(jnp.where(lane_lo == lo, qb[u], 0.0), kb[u]) for u, lo in units]
        s = [jnp.where(blocks[u][2], x, NEG_BIG) for (u, _), x in zip(units, s)]
        m = [jnp.max(x, axis=1, keepdims=True) for x in s]
        p = [jnp.exp(x - mm) for x, mm in zip(s, m)]
        l = [jnp.sum(x, axis=1, keepdims=True) for x in p]
        o = [jnp.dot(x.astype(BF16), vb[u], preferred_element_type=F32) for (u, _), x in zip(units, p)]
        for u in range(ATTN_UNROLL):
            m_b = jnp.where(lane_lo, m[2 * u], m[2 * u + 1])
            l_b = jnp.where(lane_lo, l[2 * u], l[2 * u + 1])
            o_b = jnp.where(lane_lo, o[2 * u], o[2 * u + 1])
            r = rows(blocks[u][0])
            if not first:
                m_old = m_scr[r, :]
                m_new = jnp.maximum(m_old, m_b)
                a_old = jnp.exp(m_old - m_new)
                a_new = jnp.exp(m_b - m_new)
                l_b = l_scr[r, :] * a_old + l_b * a_new
                o_b = acc_scr[r, :] * a_old + o_b * a_new
                m_b = m_new
            if last:
                o_ref[0, r, :] = o_b / l_b
            else:
                m_scr[r, :] = m_b
                l_scr[r, :] = l_b
                acc_scr[r, :] = o_b
        return carry

    lax.fori_loop(0, seq // (Q_BLOCK * ATTN_UNROLL), body, 0)


def _attn_prompt_kernel(q_ref, k_ref, v_ref, o_ref, m_scr, l_scr, acc_scr, *, seq):
    g = pl.program_id(2)
    for gi, (win, dil) in enumerate(SWA_GROUPS):
        @pl.when(g == gi)
        def _(dil=dil, nk=win // dil, gi=gi):
            _attn_group(q_ref, k_ref, v_ref, o_ref, m_scr, l_scr, acc_scr, seq=seq, dil=dil, nk=nk,
                        first=gi == 0, last=gi == N_GROUPS - 1)


def _attn_prompt(z):
    B, S, _ = z.shape
    assert Z_QKV0 % LANES == 0
    for win, dil in SWA_GROUPS:
        assert S % (dil * Q_BLOCK) == 0 and win // dil <= Q_BLOCK
    assert (S // Q_BLOCK) % ATTN_UNROLL == 0
    pairs = G_WIDTH // LANES
    blk = lambda base: pl.BlockSpec((1, S, LANES), lambda b, p, g: (b, 0, Z_QKV0 // LANES + base + g * pairs + p))
    return pl.pallas_call(
        functools.partial(_attn_prompt_kernel, seq=S),
        out_shape=jax.ShapeDtypeStruct((B, S, G_WIDTH), F32),
        grid=(B, pairs, N_GROUPS),
        in_specs=[blk(0), blk(A_WIDTH // LANES), blk(2 * A_WIDTH // LANES)],
        out_specs=pl.BlockSpec((1, S, LANES), lambda b, p, g: (b, 0, p)),
        scratch_shapes=[pltpu.VMEM((S, LANES), F32)] * 3,
        compiler_params=_cparams(("parallel", "parallel", "arbitrary")),
        name="attn_prompt",
    )(z, z, z)


def _attn_sample_kernel(q_ref, kn_ref, vn_ref, ca_ref, cb_ref, cc_ref, o_ref, *, bs):
    caches = (ca_ref, cb_ref, cc_ref)
    hrow = _iota((SUBLANES, G_WIDTH), 0)
    hmask = hrow == (_iota((SUBLANES, G_WIDTH), 1) >> 6)
    outs = []
    for n in range(bs):
        m_run = l_run = acc = None
        for gi, (win, dil) in enumerate(SWA_GROUPS):
            gs = slice(gi * G_WIDTH, (gi + 1) * G_WIDTH)
            q = q_ref[0, n:n + 1, gs] * ATTN_SCALE
            qr = jnp.where(hmask, q, 0.0).astype(BF16)
            k_t = caches[gi][0, n, 0].reshape(G_WIDTH, win).astype(BF16)
            v_t = caches[gi][0, n, 1].reshape(G_WIDTH, win).astype(BF16)
            kn = kn_ref[0, n:n + 1, gs].astype(BF16).astype(F32)
            vn = vn_ref[0, n:n + 1, gs].astype(BF16).astype(F32)
            valid = (_iota((SUBLANES, win), 1) & (dil - 1)) == 0
            s = jnp.where(valid, jnp.dot(qr, k_t, preferred_element_type=F32), NEG_BIG)
            s_n = jnp.sum(qr.astype(F32) * kn, axis=1, keepdims=True)
            m = jnp.maximum(jnp.max(s, axis=1, keepdims=True), s_n)
            p = jnp.exp(s - m)
            p_n = jnp.exp(s_n - m)
            l = jnp.sum(p, axis=1, keepdims=True) + p_n
            o = _dot_nt(p, v_t) + p_n.astype(BF16).astype(F32) * vn
            if m_run is None:
                m_run, l_run, acc = m, l, o
            else:
                m_new = jnp.maximum(m_run, m)
                a_old = jnp.exp(m_run - m_new)
                a_new = jnp.exp(m - m_new)
                l_run = l_run * a_old + l * a_new
                acc = acc * a_old + o * a_new
                m_run = m_new
        outs.append(jnp.sum(jnp.where(hmask, acc / l_run, 0.0), axis=0, keepdims=True))
    o_ref[0] = jnp.concatenate(outs, axis=0)


def _attn_sample(q, k_new, v_new, caches_t, layer, *, bs=2):
    B = q.shape[0]
    assert B % bs == 0
    for (win, dil), c in zip(SWA_GROUPS, caches_t):
        assert c.shape[-1] == win and win % dil == 0 and dil & (dil - 1) == 0
    vec = pl.BlockSpec((1, bs, A_WIDTH), lambda i: (i, 0, 0))
    cache_spec = lambda c: pl.BlockSpec((1, bs) + c.shape[2:], lambda i: (layer, i, 0, 0, 0, 0))
    rows = lambda x: x.reshape(B // bs, bs, A_WIDTH)
    out = pl.pallas_call(
        functools.partial(_attn_sample_kernel, bs=bs),
        out_shape=jax.ShapeDtypeStruct((B // bs, bs, G_WIDTH), F32),
        grid=(B // bs,),
        in_specs=[vec, vec, vec] + [cache_spec(c) for c in caches_t],
        out_specs=pl.BlockSpec((1, bs, G_WIDTH), lambda i: (i, 0, 0)),
        compiler_params=_cparams(("parallel",)),
        name="attn_sample",
    )(rows(q), rows(k_new), rows(v_new), *caches_t)
    return out.reshape(B, G_WIDTH)


def _cache_shift_kernel(new_ref, ca_ref, cb_ref, cc_ref, oa_ref, ob_ref, oc_ref, *, bs):
    rows = 2 * G_WIDTH
    for gi, (c_ref, o_ref) in enumerate(((ca_ref, oa_ref), (cb_ref, ob_ref), (cc_ref, oc_ref))):
        win = c_ref.shape[-1]
        last = _iota((rows, win), 1) == win - 1
        for n in range(bs):
            x = c_ref[0, n].reshape(rows, win)
            col = new_ref[0, gi * rows:(gi + 1) * rows, n:n + 1]
            o_ref[0, n] = jnp.where(last, col, pltpu.roll(x, win - 1, 1)).reshape(o_ref.shape[2:])


def _cache_shift(caches_t, new_cols, *, bs=2):
    depth, B = caches_t[0].shape[:2]
    assert B % bs == 0
    nb = B // bs
    spec = lambda c: pl.BlockSpec((1, bs) + c.shape[2:], lambda l, i: (l, i, 0, 0, 0, 0))
    n_rows = new_cols.shape[1]
    new_b = new_cols.reshape(depth, n_rows, nb, bs).transpose(0, 2, 1, 3).reshape(depth * nb, n_rows, bs)
    return pl.pallas_call(
        functools.partial(_cache_shift_kernel, bs=bs),
        out_shape=tuple(jax.ShapeDtypeStruct(c.shape, c.dtype) for c in caches_t),
        grid=(depth, nb),
        in_specs=[pl.BlockSpec((1, n_rows, bs), lambda l, i: (l * nb + i, 0, 0))] + [spec(c) for c in caches_t],
        out_specs=tuple(spec(c) for c in caches_t),
        compiler_params=_cparams(("parallel", "parallel")),
        name="cache_shift",
    )(new_b, *caches_t)


def _merge_kernel(gt_ref, yr_ref, yc_ref, ao_ref, x_ref, wa_ref, wo_ref, g_ref, b_ref, h_ref, *, alpha):
    d = x_ref.shape[1]
    gate = lambda i: gt_ref[:, i * d:(i + 1) * d].astype(F32)
    y_a = _dot(ao_ref[...], wa_ref[...])
    merged = gate(0) * yr_ref[...].astype(F32) + gate(1) * yc_ref[...].astype(F32) + gate(2) * y_a
    h_ref[...] = _layer_norm(alpha * x_ref[...] + _dot(merged, wo_ref[...]), g_ref[...], b_ref[...])


def _merge(gates, y_r, y_c, a_o, x, prm, *, alpha, t=512):
    M, D = x.shape
    t = min(t, M)
    assert M % t == 0
    row = lambda a: a.reshape(1, -1)
    tile = lambda a: pl.BlockSpec((t, a.shape[1]), lambda i: (i, 0))
    full = lambda a: pl.BlockSpec(a.shape, lambda i: (0, 0))
    acts = [gates, y_r, y_c, a_o, x]
    params = [prm["w_attn_out_bf"], prm["w_o_bf"], row(prm["ln1_g"]), row(prm["ln1_b"])]
    return pl.pallas_call(
        functools.partial(_merge_kernel, alpha=alpha),
        out_shape=jax.ShapeDtypeStruct((M, D), F32),
        grid=(M // t,),
        in_specs=[tile(a) for a in acts] + [full(a) for a in params],
        out_specs=pl.BlockSpec((t, D), lambda i: (i, 0)),
        compiler_params=_cparams(("parallel",)),
        name="merge",
    )(*acts, *params)


def _ffn_kernel(*refs, alpha, t, tiles_per_seq, sample):
    if sample:
        (h_ref, pg_ref, pu_ref, wg_ref, wu_ref, dg_ref, du_ref, bg_ref, bu_ref, wo_ref, g_ref, b_ref,
         o_ref, ng_ref, nu_ref, acc_scr) = refs
    else:
        (h_ref, wg_ref, wu_ref, dg_ref, du_ref, bg_ref, bu_ref, wo_ref, g_ref, b_ref,
         o_ref, ng_ref, nu_ref, acc_scr, cg_scr, cu_scr) = refs
    i = pl.program_id(0)
    j = pl.program_id(1)
    nj = pl.num_programs(1)
    hb = h_ref[...].astype(BF16)

    def conv(u, dw_ref, bias_ref, p1, p2):
        return dw_ref[0:1, :] * p2 + dw_ref[1:2, :] * p1 + dw_ref[2:3, :] * u + bias_ref[...]

    def branch(w_ref, dw_ref, bias_ref, new_ref, prev_ref_or_scr):
        u = jnp.dot(hb, w_ref[...], preferred_element_type=F32)
        if sample:
            new_ref[...] = u
            p2, p1 = prev_ref_or_scr[0], prev_ref_or_scr[1]
        else:
            new_ref[0] = u[t - 2:t, :]
            row = _iota((SUBLANES, u.shape[1]), 0)
            fresh = (i % tiles_per_seq) == 0
            c1 = jnp.where(fresh, 0.0, prev_ref_or_scr[j, 7:8, :])
            c2 = jnp.where(fresh, 0.0, prev_ref_or_scr[j, 6:7, :])
            r1 = pltpu.roll(u, 1, 0)
            r2 = pltpu.roll(u, 2, 0)
            p1 = jnp.concatenate([jnp.where(row == 0, c1, r1[:SUBLANES]), r1[SUBLANES:]], axis=0)
            p2 = jnp.concatenate([jnp.where(row == 0, c2, jnp.where(row == 1, c1, r2[:SUBLANES])),
                                  r2[SUBLANES:]], axis=0)
            prev_ref_or_scr[j, :, :] = u[t - 8:t, :]
        return conv(u, dw_ref, bias_ref, p1, p2)

    if sample:
        y_g = branch(wg_ref, dg_ref, bg_ref, ng_ref, pg_ref)
        y_u = branch(wu_ref, du_ref, bu_ref, nu_ref, pu_ref)
    else:
        y_g = branch(wg_ref, dg_ref, bg_ref, ng_ref, cg_scr)
        y_u = branch(wu_ref, du_ref, bu_ref, nu_ref, cu_scr)
    act = (y_g * _sigmoid(y_g)) * y_u
    part = _dot(act, wo_ref[...])

    @pl.when(j == 0)
    def _():
        acc_scr[...] = part

    @pl.when(j > 0)
    def _():
        acc_scr[...] += part

    @pl.when(j == nj - 1)
    def _():
        o_ref[...] = _layer_norm(alpha * h_ref[...] + acc_scr[...], g_ref[...], b_ref[...])


def _ffn(h, prm, *, alpha, seq_len, prev=None, t=512, tn=1408):
    M, D = h.shape
    d_ff = prm["w_ffn_out_bf"].shape[0]
    sample = prev is not None
    t = M if sample else min(t, seq_len)
    tn = min(tn, d_ff)
    assert M % t == 0 and d_ff % tn == 0 and (sample or (seq_len % t == 0 and t >= 8))
    J = d_ff // tn
    row = lambda a: a.reshape(1, -1)
    w_in, dw, dwb = prm["w_ffn_in_bf"], prm["ffn_dw"], row(prm["ffn_dw_b"])
    col_g = lambda shape: pl.BlockSpec(shape, lambda i, j: (0, j))
    col_u = lambda shape: pl.BlockSpec(shape, lambda i, j: (0, J + j))
    const = lambda a: pl.BlockSpec(a.shape, lambda i, j: (0, 0))
    ins, specs = [h], [pl.BlockSpec((t, D), lambda i, j: (i, 0))]
    if sample:
        pstack = jnp.stack(prev)
        ins += [pstack, pstack]
        specs += [pl.BlockSpec((2, M, tn), lambda i, j: (0, 0, j)), pl.BlockSpec((2, M, tn), lambda i, j: (0, 0, J + j))]
    ins += [w_in, w_in, dw, dw, dwb, dwb, prm["w_ffn_out_bf"], row(prm["ln2_g"]), row(prm["ln2_b"])]
    specs += [col_g((D, tn)), col_u((D, tn)), col_g((FFN_K, tn)), col_u((FFN_K, tn)), col_g((1, tn)),
              col_u((1, tn)), pl.BlockSpec((tn, D), lambda i, j: (j, 0)), const(row(prm["ln2_g"])),
              const(row(prm["ln2_b"]))]
    if sample:
        new_shape = jax.ShapeDtypeStruct((M, d_ff), F32)
        new_spec = pl.BlockSpec((M, tn), lambda i, j: (0, j))
        scratch = [pltpu.VMEM((t, D), F32)]
        tiles_per_seq = 1
    else:
        tiles_per_seq = seq_len // t
        new_shape = jax.ShapeDtypeStruct((M // t, 2, d_ff), F32)
        new_spec = pl.BlockSpec((1, 2, tn), lambda i, j: (i, 0, j))
        scratch = [pltpu.VMEM((t, D), F32), pltpu.VMEM((J, 8, tn), F32), pltpu.VMEM((J, 8, tn), F32)]
    out, new_g, new_u = pl.pallas_call(
        functools.partial(_ffn_kernel, alpha=alpha, t=t, tiles_per_seq=tiles_per_seq, sample=sample),
        out_shape=(jax.ShapeDtypeStruct((M, D), F32), new_shape, new_shape),
        grid=(M // t, J),
        in_specs=specs,
        out_specs=(pl.BlockSpec((t, D), lambda i, j: (i, 0)), new_spec, new_spec),
        scratch_shapes=scratch,
        compiler_params=_cparams(("arbitrary", "arbitrary")),
        name="ffn_sample" if sample else "ffn_prompt",
    )(*ins)
    if not sample:
        new_g = new_g[tiles_per_seq - 1::tiles_per_seq]
        new_u = new_u[tiles_per_seq - 1::tiles_per_seq]
    return out, new_g, new_u


def _layer_weights(l, w_in, w_rwkv_out, w_conv_out, w_attn_out, w_o, w_ffn_in, w_ffn_out, small):
    prm = {k: v[l] for k, v in small.items()}
    wi = w_in[l]
    o_c = R_COLS
    o_q = o_c + 2 * CONV_CH
    o_g = o_q + 3 * A_WIDTH
    prm["w_in_bf"] = jnp.concatenate([wi[:, o_g:], wi[:, :o_c], wi[:, o_q:o_g], wi[:, o_c:o_q]], axis=1).astype(BF16)
    prm["w_rwkv_out_bf"] = w_rwkv_out[l].astype(BF16)
    prm["w_conv_out_bf"] = w_conv_out[l].astype(BF16)
    prm["w_attn_out_bf"] = w_attn_out[l].astype(BF16)
    prm["w_o_bf"] = w_o[l].astype(BF16)
    prm["w_ffn_in_bf"] = w_ffn_in[l].astype(BF16)
    prm["w_ffn_out_bf"] = w_ffn_out[l].astype(BF16)
    return prm


def _project_kernel(x_ref, w_ref, gb_ref, g_ref, z_ref, *, n_gate):
    j = pl.program_id(1)
    acc = jnp.dot(x_ref[...].astype(BF16), w_ref[...], preferred_element_type=F32)

    @pl.when(j < n_gate)
    def _():
        g_ref[...] = _sigmoid(acc + gb_ref[...]).astype(g_ref.dtype)

    @pl.when(j >= n_gate)
    def _():
        z_ref[...] = acc


def _project(x2, prm, tm, tn=1024):
    M, K = x2.shape
    w = prm["w_in_bf"]
    gate_w = prm["gate_b"].size
    tm = min(tm, M)
    assert M % tm == 0 and gate_w % tn == 0 and Z_WIDTH % tn == 0 and w.shape[1] == gate_w + Z_WIDTH
    n_gate = gate_w // tn
    return pl.pallas_call(
        functools.partial(_project_kernel, n_gate=n_gate),
        out_shape=(jax.ShapeDtypeStruct((M, gate_w), BF16), jax.ShapeDtypeStruct((M, Z_WIDTH), F32)),
        grid=(M // tm, w.shape[1] // tn),
        in_specs=[pl.BlockSpec((tm, K), lambda i, j: (i, 0)),
                  pl.BlockSpec((K, tn), lambda i, j: (0, j)),
                  pl.BlockSpec((1, tn), lambda i, j: (0, jnp.minimum(j, n_gate - 1)))],
        out_specs=(pl.BlockSpec((tm, tn), lambda i, j: (i, jnp.minimum(j, n_gate - 1))),
                   pl.BlockSpec((tm, tn), lambda i, j: (i, jnp.maximum(j - n_gate, 0)))),
        compiler_params=_cparams(("parallel", "arbitrary")),
        name="project",
    )(x2, w, prm["gate_b"].reshape(1, -1))


def _kv_rows(z_qkv, gi):
    k = z_qkv[..., A_WIDTH + gi * G_WIDTH:A_WIDTH + (gi + 1) * G_WIDTH]
    v = z_qkv[..., 2 * A_WIDTH + gi * G_WIDTH:2 * A_WIDTH + (gi + 1) * G_WIDTH]
    kv = jnp.stack([k, v], axis=-2)
    return kv.reshape(kv.shape[:-1] + (G_HEADS, HEAD_DIM))


def _prompt_layer(x, prm, alpha):
    B, L, D = x.shape
    x2 = x.reshape(B * L, D)
    gates, z = _project(x2, prm, tm=1024)
    z = z.reshape(B, L, Z_WIDTH)
    y_r, new_wkv = _rwkv_prompt(z, prm)
    y_c, new_conv = _conv_prompt(z, prm)
    a_o = _attn_prompt(z)
    h = _merge(gates, y_r.reshape(B * L, D), y_c.reshape(B * L, D), a_o.reshape(B * L, -1), x2, prm, alpha=alpha)
    out, nf_g, nf_u = _ffn(h, prm, alpha=alpha, seq_len=L)
    new_kv = [_kv_rows(z[:, L - min(win, L):, Z_QKV0:Z_C0], gi) for gi, (win, _) in enumerate(SWA_GROUPS)]
    return (out.reshape(B, L, D), z[:, L - 1:, :R_COLS], new_wkv, new_conv, new_kv,
            jnp.concatenate([nf_g, nf_u], axis=-1))


def _sample_layer(x, prm, alpha, shift, wkv, conv_buf, ffn_buf, caches_t, layer):
    B, L, D = x.shape
    assert L == 1
    x2 = x.reshape(B, D)
    z_g, z = _project(x2, prm, tm=B)
    z_r, z_qkv, z_c = z[:, :Z_QKV0], z[:, Z_QKV0:Z_C0], z[:, Z_C0:]
    o_r, new_wkv = _rwkv_sample(z_r, shift.reshape(B, R_COLS), wkv, prm)
    y_r = _mm(o_r, prm["w_rwkv_out_bf"], tm=B, tn=1024, out_dtype=BF16)
    c, u = _conv_sample(z_c, conv_buf, prm)
    y_c = _mm(c, prm["w_conv_out_bf"], tm=B, tn=1024, out_dtype=BF16)
    a_o = _attn_sample(z_qkv[:, :A_WIDTH], z_qkv[:, A_WIDTH:2 * A_WIDTH], z_qkv[:, 2 * A_WIDTH:], caches_t, layer)
    h = _merge(z_g, y_r, y_c, a_o, x2, prm, alpha=alpha)
    out, u_g, u_u = _ffn(h, prm, alpha=alpha, seq_len=1, prev=(ffn_buf[:, 0], ffn_buf[:, 1]))
    kv_new = []
    for gi in range(N_GROUPS):
        kv_new += [z_qkv[:, A_WIDTH + gi * G_WIDTH:A_WIDTH + (gi + 1) * G_WIDTH],
                   z_qkv[:, 2 * A_WIDTH + gi * G_WIDTH:2 * A_WIDTH + (gi + 1) * G_WIDTH]]
    return (out.reshape(B, 1, D), z_r.reshape(B, 1, R_COLS), new_wkv, u, jnp.concatenate(kv_new, axis=-1),
            jnp.concatenate([u_g, u_u], axis=-1))


def _append_row(buf, rows):
    return jnp.concatenate([buf[:, :, 1:], rows[:, :, None]], axis=2)


def kernel(x_prompt, x_sample, state_shift, state_wkv, state_conv, cache_swa_a, cache_swa_b, cache_swa_c, state_ffn, w_in, rwkv_mu, rwkv_w0, rwkv_w_up, rwkv_a0, rwkv_a_up, rwkv_g_up, rwkv_k_k, rwkv_k_a, rwkv_r_k, rwkv_ln_g, rwkv_ln_b, w_rwkv_out, conv_dw, conv_dw_b, conv_ln_g, conv_ln_b, w_conv_out, w_attn_out, gate_b, w_o, ln1_g, ln1_b, w_ffn_in, ffn_dw, ffn_dw_b, w_ffn_out, ln2_g, ln2_b):
    depth = w_in.shape[0]
    alpha = (2 * depth) ** 0.25
    small = dict(rwkv_mu=rwkv_mu, rwkv_w0=rwkv_w0, rwkv_w_up=rwkv_w_up, rwkv_a0=rwkv_a0, rwkv_a_up=rwkv_a_up,
                 rwkv_g_up=rwkv_g_up, rwkv_k_k=rwkv_k_k, rwkv_k_a=rwkv_k_a,
                 rwkv_r_k=rwkv_r_k.reshape(depth, -1), rwkv_ln_g=rwkv_ln_g, rwkv_ln_b=rwkv_ln_b,
                 conv_dw=conv_dw, conv_dw_b=conv_dw_b, conv_ln_g=conv_ln_g, conv_ln_b=conv_ln_b,
                 gate_b=gate_b.reshape(depth, -1), ln1_g=ln1_g, ln1_b=ln1_b, ffn_dw=ffn_dw, ffn_dw_b=ffn_dw_b,
                 ln2_g=ln2_g, ln2_b=ln2_b)
    caches = (cache_swa_a, cache_swa_b, cache_swa_c)
    caches_t = [jnp.transpose(c, (0, 1, 3, 4, 5, 2)) for c in caches]
    hp, hs = x_prompt, x_sample
    outs_p, outs_s = [], []
    for l in range(depth):
        prm = _layer_weights(l, w_in, w_rwkv_out, w_conv_out, w_attn_out, w_o, w_ffn_in, w_ffn_out, small)
        res_p = _prompt_layer(hp, prm, alpha)
        hp = res_p[0]
        outs_p.append(res_p[1:])
        res_s = _sample_layer(hs, prm, alpha, state_shift[l], state_wkv[l], state_conv[l], state_ffn[l],
                              caches_t, l)
        hs = res_s[0]
        outs_s.append(res_s[1:])
    stk = lambda outs, f: jnp.stack([f(o) for o in outs])
    res = [hp, hs]
    res += [stk(outs_p, lambda o: o[0]), stk(outs_s, lambda o: o[0])]
    res += [stk(outs_p, lambda o: o[1]), stk(outs_s, lambda o: o[1])]
    res += [stk(outs_p, lambda o: o[2]), _append_row(state_conv, stk(outs_s, lambda o: o[2]))]
    new_cols = jnp.transpose(stk(outs_s, lambda o: o[3]), (0, 2, 1))
    shifted = _cache_shift(caches_t, new_cols)
    for gi in range(N_GROUPS):
        res += [stk(outs_p, lambda o: o[3][gi]), jnp.transpose(shifted[gi], (0, 1, 5, 2, 3, 4))]
    res += [stk(outs_p, lambda o: o[4]), _append_row(state_ffn, stk(outs_s, lambda o: o[4]))]
    return tuple(res)
```

```python
import functools

import jax
import jax.numpy as jnp
from jax import lax
from jax.experimental import pallas as pl
from jax.experimental.pallas import tpu as pltpu

F32 = jnp.float32
BF16 = jnp.bfloat16

HEAD_DIM = 64
R_HEADS = 8
R_WIDTH = R_HEADS * HEAD_DIM
DECAY_LORA = 64
ICLR_LORA = 64
GATE_LORA = 128
R_COLS = 3 * R_WIDTH + DECAY_LORA + ICLR_LORA + GATE_LORA
LNX_EPS = 64e-5
CONV_CH = 512
CONV_K = 31
SWA_GROUPS = ((128, 1), (512, 4), (2048, 16))
N_GROUPS = len(SWA_GROUPS)
G_HEADS = 4
G_WIDTH = G_HEADS * HEAD_DIM
A_WIDTH = N_GROUPS * G_WIDTH
Q_BLOCK = 128
ATTN_SCALE = HEAD_DIM ** -0.5
N_BRANCH = 3
FFN_K = 3
LN_EPS = 1e-5
NEG_BIG = -1e30

Z_QKV0 = R_COLS
Z_C0 = Z_QKV0 + 3 * A_WIDTH
Z_WIDTH = Z_C0 + 2 * CONV_CH

LANES = 128
CHUNK = 64
SUBLANES = 8
CONV_ROWS = 32
SEG_BLOCK = 256
FFN_ROW_PARTS = 2
ATTN_UNROLL = 8
VMEM_LIMIT = 56 * 1024 * 1024


def _cparams(sem):
    return pltpu.CompilerParams(dimension_semantics=sem, vmem_limit_bytes=VMEM_LIMIT)


def _dot(a, b):
    return jnp.dot(a.astype(BF16), b.astype(BF16), preferred_element_type=F32)


def _dot_nt(a, b):
    return lax.dot_general(a.astype(BF16), b.astype(BF16), (((1,), (1,)), ((), ())),
                           preferred_element_type=F32)


def _dot_tn(a, b):
    return lax.dot_general(a.astype(BF16), b.astype(BF16), (((0,), (0,)), ((), ())),
                           preferred_element_type=F32)


def _split3(x):
    h1 = x.astype(BF16)
    r1 = x - h1.astype(F32)
    h2 = r1.astype(BF16)
    h3 = (r1 - h2.astype(F32)).astype(BF16)
    return h1, h2, h3


def _sigmoid(x):
    return 1.0 / (1.0 + jnp.exp(-x))


def _layer_norm(x, g, b):
    mu = jnp.mean(x, axis=-1, keepdims=True)
    xc = x - mu
    var = jnp.mean(xc * xc, axis=-1, keepdims=True)
    return xc * lax.rsqrt(var + LN_EPS) * g + b


def _iota(shape, dim):
    return lax.broadcasted_iota(jnp.int32, shape, dim)


def _head_ones():
    r = _iota((SEG_BLOCK, SEG_BLOCK), 0)
    c = _iota((SEG_BLOCK, SEG_BLOCK), 1)
    return jnp.where((r >> 6) == (c >> 6), 1.0, 0.0).astype(BF16)


def _segsum(x, ones):
    outs = [_dot(x[:, p:p + SEG_BLOCK], ones) for p in range(0, x.shape[1], SEG_BLOCK)]
    return outs[0] if len(outs) == 1 else jnp.concatenate(outs, axis=1)


def _mm_kernel(x_ref, w_ref, o_ref):
    o_ref[...] = jnp.dot(x_ref[...].astype(BF16), w_ref[...],
                         preferred_element_type=F32).astype(o_ref.dtype)


def _mm(x, w, *, tm, tn, out_dtype=F32):
    M, K = x.shape
    N = w.shape[1]
    tm = min(tm, M)
    tn = min(tn, N)
    assert M % tm == 0 and N % tn == 0
    return pl.pallas_call(
        _mm_kernel,
        out_shape=jax.ShapeDtypeStruct((M, N), out_dtype),
        grid=(M // tm, N // tn),
        in_specs=[pl.BlockSpec((tm, K), lambda i, j: (i, 0)),
                  pl.BlockSpec((K, tn), lambda i, j: (0, j))],
        out_specs=pl.BlockSpec((tm, tn), lambda i, j: (i, j)),
        compiler_params=_cparams(("parallel", "parallel")),
        name="mm",
    )(x, w)


def _rwkv_prep(zr, prevs, mu, w0, w_up, a0, a_up, g_up, k_k, k_a, ones):
    zs = zr + (prevs - zr) * mu
    r = zs[:, 0:R_WIDTH]
    k = zs[:, R_WIDTH:2 * R_WIDTH]
    v = zs[:, 2 * R_WIDTH:3 * R_WIDTH]
    o1 = 3 * R_WIDTH
    wd = zs[:, o1:o1 + DECAY_LORA]
    ad = zs[:, o1 + DECAY_LORA:o1 + DECAY_LORA + ICLR_LORA]
    gd = zs[:, o1 + DECAY_LORA + ICLR_LORA:R_COLS]
    nx = -(w0 + _dot(jnp.tanh(wd), w_up))
    softplus = jnp.maximum(nx, 0.0) + jnp.log(1.0 + jnp.exp(-jnp.abs(nx)))
    logdec = -jnp.exp(-softplus - 0.5)
    a = _sigmoid(a0 + _dot(ad, a_up))
    g = _dot(_sigmoid(gd), g_up)
    kk = k * k_k
    kk = kk / jnp.maximum(jnp.sqrt(_segsum(kk * kk, ones)), 1e-12)
    k2 = k * (1.0 + (a - 1.0) * k_a)
    return r, k2, v, kk, a, g, logdec


def _rwkv_post(o, r, k2, v, g, r_k, ln_g, ln_b, ones):
    inv = 1.0 / HEAD_DIM
    m = _segsum(o, ones) * inv
    oc = o - m
    var = _segsum(oc * oc, ones) * inv
    on = oc * lax.rsqrt(var + LNX_EPS) * ln_g + ln_b
    bonus = _segsum(r * k2 * r_k, ones) * v
    return (on + bonus) * g


def _chunk_cumsum(logdec, tri_bd):
    h1, h2, h3 = _split3(logdec)
    d = lambda x: jnp.dot(tri_bd, x, preferred_element_type=F32)
    return d(h1) + d(h2) + d(h3)


def _rwkv_chunk_tables(logdec, cum_all, r, k2, v, kk, a, c):
    sl = slice(c * CHUNK, (c + 1) * CHUNK)
    cum = cum_all[sl]
    ld = logdec[sl]
    cum_end = cum[CHUNK - 1:CHUNK, :]
    p_in = jnp.exp(cum)
    p_ex = jnp.exp(cum - ld)
    p_inv = jnp.exp(-cum)
    p_tail = jnp.exp(cum_end - cum)
    kka = kk[sl] * a[sl]
    return dict(rt=r[sl] * p_in, at=-kk[sl] * p_ex, bt=kka * p_inv, kt=k2[sl] * p_inv,
                bh=kka * p_tail, kh=k2[sl] * p_tail, v=v[sl], pc=jnp.exp(cum_end))


def _rwkv_units(tables, n_pairs, masks):
    strict, incl, eye, lane_lo, row_lo, lane_lo_n = masks
    n = 2 * CHUNK
    units = [(tb, slice(p * LANES, (p + 1) * LANES)) for tb in tables for p in range(n_pairs)]
    swap = lambda x: pltpu.roll(x, HEAD_DIM, 1)
    masked = lambda x: jnp.concatenate([jnp.where(lane_lo, x, 0.0), jnp.where(lane_lo, 0.0, x)], axis=0)
    local_lo = lambda x: jnp.concatenate([x, swap(x)], axis=0)
    local_hi = lambda x: jnp.concatenate([swap(x), x], axis=0)
    twice = lambda x: jnp.concatenate([x, x], axis=0)

    def spread(x):
        return jnp.where(row_lo, jnp.where(lane_lo_n, x, 0.0), jnp.where(lane_lo_n, 0.0, swap(x)))

    aa, a_lo, v_hi, r_m, bk_m = [], [], [], [], []
    for tb, ls in units:
        r_m.append(masked(tb["rt"][:, ls]))
        aa.append(_dot_nt(jnp.concatenate([masked(tb["at"][:, ls]), r_m[-1]], axis=0),
                          jnp.concatenate([twice(tb["bt"][:, ls]), twice(tb["kt"][:, ls])], axis=0)))
        a_lo.append(local_lo(tb["at"][:, ls]))
        v_hi.append(local_hi(tb["v"][:, ls]))
        bk_m.append(jnp.concatenate([masked(tb["bh"][:, ls]), masked(tb["kh"][:, ls])], axis=0))
    lj = [jnp.where(strict, a[:n, :n], 0.0) for a in aa]
    x = [jnp.where(lane_lo_n, al, _dot(jnp.where(strict, a[:n, n:], 0.0), vh))
         for al, vh, a in zip(a_lo, v_hi, aa)]
    steps = CHUNK.bit_length() - 1
    for j in range(steps):
        if j < steps - 1:
            prod = [_dot(l, jnp.concatenate([l, xx], axis=1)) for l, xx in zip(lj, x)]
            lj = [pr[:, :n] for pr in prod]
            x = [xx + pr[:, n:] for xx, pr in zip(x, prod)]
        else:
            x = [xx + _dot(l, xx) for l, xx in zip(lj, x)]
    rhs = [jnp.concatenate([xx, jnp.where(lane_lo_n, 0.0, vh)], axis=0) for xx, vh in zip(x, v_hi)]
    qo = [_dot(jnp.concatenate([jnp.where(incl, a[n:, :n], 0.0), jnp.where(incl, a[n:, n:], 0.0)], axis=1), rh)
          for a, rh in zip(aa, rhs)]
    gh = [_dot_tn(bk, rh) for bk, rh in zip(bk_m, rhs)]
    return [(rm + spread(q), q, spread(g) + jnp.where(eye, tb["pc"][:, ls], 0.0), g)
            for (tb, ls), rm, q, g in zip(units, r_m, qo, gh)]


def _rwkv_masks():
    n = 2 * CHUNK
    rr = _iota((n, n), 0)
    cc = _iota((n, n), 1)
    same = (rr >> 6) == (cc >> 6)
    strict = same & ((cc & 63) < (rr & 63))
    incl = same & ((cc & 63) <= (rr & 63))
    eye = rr == cc
    lane_lo = _iota((CHUNK, LANES), 1) < HEAD_DIM
    return strict, incl, eye, lane_lo, rr < CHUNK, cc < HEAD_DIM


def _rwkv_prompt_kernel(z_ref, mu_ref, w0_ref, wup_ref, a0_ref, aup_ref, gup_ref, kk_ref, ka_ref,
                        rk_ref, lng_ref, lnb_ref, wout_ref, y_ref, st_ref, prev_scr, st_scr, *, ct):
    i = pl.program_id(1)

    @pl.when(i == 0)
    def _():
        prev_scr[...] = jnp.zeros_like(prev_scr)
        st_scr[...] = jnp.zeros_like(st_scr)

    zr = z_ref[0]
    row = _iota(zr.shape, 0)
    prevs = jnp.where(row == 0, prev_scr[0:1, :], pltpu.roll(zr, 1, 0))
    prev_scr[0:1, :] = zr[ct - 1:ct, :]
    ones = _head_ones()
    r, k2, v, kk, a, g, logdec = _rwkv_prep(
        zr, prevs, mu_ref[...], w0_ref[...], wup_ref[...], a0_ref[...], aup_ref[...], gup_ref[...],
        kk_ref[...], ka_ref[...], ones)

    rr = _iota((ct, ct), 0)
    cc = _iota((ct, ct), 1)
    tri_bd = jnp.where(((rr >> 6) == (cc >> 6)) & (cc <= rr), 1.0, 0.0).astype(BF16)
    masks = _rwkv_masks()
    n_pairs = R_WIDTH // LANES
    cum_all = _chunk_cumsum(logdec, tri_bd)
    n_chunks = ct // CHUNK
    tables = [_rwkv_chunk_tables(logdec, cum_all, r, k2, v, kk, a, c) for c in range(n_chunks)]
    units = _rwkv_units(tables, n_pairs, masks)
    states = [st_scr[p] for p in range(n_pairs)]
    lane_lo, lane_lo_n = masks[3], masks[5]
    o_rows = []
    for c in range(n_chunks):
        cur = units[c * n_pairs:(c + 1) * n_pairs]
        ostk = [_dot(qp, st) + op for (qp, op, _, _), st in zip(cur, states)]
        states = [jnp.where(lane_lo_n, 0.0, _dot(gm, st) + hm) for (_, _, gm, hm), st in zip(cur, states)]
        o_rows.append(jnp.concatenate(
            [jnp.where(lane_lo, pltpu.roll(o[:CHUNK], HEAD_DIM, 1), o[CHUNK:]) for o in ostk], axis=1))
    for p in range(n_pairs):
        st_scr[p] = states[p]
    o = o_rows[0] if len(o_rows) == 1 else jnp.concatenate(o_rows, axis=0)
    out = _rwkv_post(o, r, k2, v, g, rk_ref[...], lng_ref[...], lnb_ref[...], ones)
    y_ref[0] = _dot(out, wout_ref[...]).astype(y_ref.dtype)
    st_ref[0] = st_scr[...]


def _rwkv_prompt(z_r, prm, *, ct=256):
    B, L, _ = z_r.shape
    assert L % ct == 0 and ct % CHUNK == 0
    n_pairs = R_WIDTH // LANES
    row = lambda x: x.reshape(1, -1)
    full = lambda a: pl.BlockSpec(a.shape, lambda b, i: (0,) * a.ndim)
    params = [row(prm["rwkv_mu"]), row(prm["rwkv_w0"]), prm["rwkv_w_up"], row(prm["rwkv_a0"]),
              prm["rwkv_a_up"], prm["rwkv_g_up"], row(prm["rwkv_k_k"]), row(prm["rwkv_k_a"]),
              row(prm["rwkv_r_k"]), row(prm["rwkv_ln_g"]), row(prm["rwkv_ln_b"]), prm["w_rwkv_out_bf"]]
    y, st = pl.pallas_call(
        functools.partial(_rwkv_prompt_kernel, ct=ct),
        out_shape=(jax.ShapeDtypeStruct((B, L, prm["w_rwkv_out_bf"].shape[1]), BF16),
                   jax.ShapeDtypeStruct((B, n_pairs, LANES, LANES), F32)),
        grid=(B, L // ct),
        in_specs=[pl.BlockSpec((1, ct, R_COLS), lambda b, i: (b, i, 0))] + [full(a) for a in params],
        out_specs=(pl.BlockSpec((1, ct, prm["w_rwkv_out_bf"].shape[1]), lambda b, i: (b, i, 0)),
                   pl.BlockSpec((1, n_pairs, LANES, LANES), lambda b, i: (b, 0, 0, 0))),
        scratch_shapes=[pltpu.VMEM((8, R_COLS), F32), pltpu.VMEM((n_pairs, LANES, LANES), F32)],
        compiler_params=_cparams(("parallel", "arbitrary")),
        name="rwkv_prompt",
    )(z_r, *params)
    st = st[..., HEAD_DIM:].reshape(B, R_HEADS, HEAD_DIM, HEAD_DIM)
    return y, jnp.swapaxes(st, -1, -2)


def _rwkv_sample_kernel(z_ref, sh_ref, s_ref, mu_ref, w0_ref, wup_ref, a0_ref, aup_ref, gup_ref,
                        kk_ref, ka_ref, rk_ref, lng_ref, lnb_ref, o_ref, sn_ref):
    ones = _head_ones()
    zr = z_ref[0]
    r, k2, v, kk, a, g, logdec = _rwkv_prep(
        zr, sh_ref[0], mu_ref[...], w0_ref[...], wup_ref[...], a0_ref[...], aup_ref[...],
        gup_ref[...], kk_ref[...], ka_ref[...], ones)
    w = jnp.exp(logdec)
    b = kk * a
    eye = _iota((HEAD_DIM, HEAD_DIM), 0) == _iota((HEAD_DIM, HEAD_DIM), 1)
    col = lambda x: jnp.sum(jnp.where(eye, x, 0.0), axis=1, keepdims=True)
    o_heads = []
    for h in range(R_HEADS):
        hs = slice(h * HEAD_DIM, (h + 1) * HEAD_DIM)
        s = s_ref[0, h]
        sa = jnp.sum(s * (-kk[:, hs]), axis=1, keepdims=True)
        s_new = s * w[:, hs] + sa * b[:, hs] + col(v[:, hs]) * k2[:, hs]
        sn_ref[0, h] = s_new
        o_col = jnp.sum(s_new * r[:, hs], axis=1, keepdims=True)
        o_heads.append(jnp.sum(jnp.where(eye, o_col, 0.0), axis=0, keepdims=True))
    o = jnp.concatenate(o_heads, axis=1)
    o_ref[0] = _rwkv_post(o, r, k2, v, g, rk_ref[...], lng_ref[...], lnb_ref[...], ones)


def _rwkv_sample(z_r, shift, wkv, prm):
    B = z_r.shape[0]
    row = lambda x: x.reshape(1, -1)
    full = lambda a: pl.BlockSpec(a.shape, lambda b: (0,) * a.ndim)
    params = [row(prm["rwkv_mu"]), row(prm["rwkv_w0"]), prm["rwkv_w_up"], row(prm["rwkv_a0"]),
              prm["rwkv_a_up"], prm["rwkv_g_up"], row(prm["rwkv_k_k"]), row(prm["rwkv_k_a"]),
              row(prm["rwkv_r_k"]), row(prm["rwkv_ln_g"]), row(prm["rwkv_ln_b"])]
    o, s_new = pl.pallas_call(
        _rwkv_sample_kernel,
        out_shape=(jax.ShapeDtypeStruct((B, 1, R_WIDTH), F32),
                   jax.ShapeDtypeStruct(wkv.shape, F32)),
        grid=(B,),
        in_specs=[pl.BlockSpec((1, 1, R_COLS), lambda b: (b, 0, 0)),
                  pl.BlockSpec((1, 1, R_COLS), lambda b: (b, 0, 0)),
                  pl.BlockSpec((1,) + wkv.shape[1:], lambda b: (b, 0, 0, 0))] + [full(a) for a in params],
        out_specs=(pl.BlockSpec((1, 1, R_WIDTH), lambda b: (b, 0, 0)),
                   pl.BlockSpec((1,) + wkv.shape[1:], lambda b: (b, 0, 0, 0))),
        compiler_params=_cparams(("parallel",)),
        name="rwkv_sample",
    )(z_r.reshape(B, 1, R_COLS), shift.reshape(B, 1, R_COLS), wkv, *params)
    return o.reshape(B, R_WIDTH), s_new


def _conv_prompt_kernel(z_ref, dw_ref, dwb_ref, lng_ref, lnb_ref, wout_ref, y_ref, nb_ref, u_scr, sh_scr,
                        acc_scr, *, t):
    i = pl.program_id(1)
    pad = 32

    @pl.when(i == 0)
    def _():
        u_scr[0:pad, :] = jnp.zeros((pad, CONV_CH), F32)
        u_scr[pad + t:, :] = jnp.zeros((SUBLANES, CONV_CH), F32)

    z = z_ref[0]
    u_scr[pad:pad + t, :] = z[:, :CONV_CH] * _sigmoid(z[:, CONV_CH:])
    span = t + pad
    for s in range(SUBLANES):
        sh_scr[s, 0:span, :] = u_scr[s:s + span, :]
    off = pad - (CONV_K - 1)

    def rows_block(bi, carry):
        r0 = pl.multiple_of(bi * CONV_ROWS, CONV_ROWS)
        acc = jnp.zeros((CONV_ROWS, CONV_CH), F32) + dwb_ref[...]
        for j in range(CONV_K):
            a, s = divmod(off + j, SUBLANES)
            acc = acc + dw_ref[j:j + 1, :] * sh_scr[s, pl.ds(r0 + a * SUBLANES, CONV_ROWS), :]
        acc_scr[pl.ds(r0, CONV_ROWS), :] = acc
        return carry

    lax.fori_loop(0, t // CONV_ROWS, rows_block, 0)
    tail = u_scr[t:t + pad, :]
    nb_ref[0] = tail
    u_scr[0:pad, :] = tail
    c = _layer_norm(acc_scr[...], lng_ref[...], lnb_ref[...])
    c = c * _sigmoid(c)
    y_ref[0] = _dot(c, wout_ref[...]).astype(y_ref.dtype)


def _conv_prompt(z, prm, *, t=512):
    B, L, _ = z.shape
    assert Z_C0 % (2 * CONV_CH) == 0
    t = min(t, L)
    assert L % t == 0 and t >= 32 and t % CONV_ROWS == 0
    row = lambda x: x.reshape(1, -1)
    full = lambda a: pl.BlockSpec(a.shape, lambda b, i: (0,) * a.ndim)
    dw = jnp.pad(prm["conv_dw"], ((0, 32 - CONV_K), (0, 0)))
    params = [dw, row(prm["conv_dw_b"]), row(prm["conv_ln_g"]), row(prm["conv_ln_b"]), prm["w_conv_out_bf"]]
    d_out = prm["w_conv_out_bf"].shape[1]
    y, nb = pl.pallas_call(
        functools.partial(_conv_prompt_kernel, t=t),
        out_shape=(jax.ShapeDtypeStruct((B, L, d_out), BF16), jax.ShapeDtypeStruct((B, 32, CONV_CH), F32)),
        grid=(B, L // t),
        in_specs=[pl.BlockSpec((1, t, 2 * CONV_CH), lambda b, i: (b, i, Z_C0 // (2 * CONV_CH)))]
                 + [full(a) for a in params],
        out_specs=(pl.BlockSpec((1, t, d_out), lambda b, i: (b, i, 0)),
                   pl.BlockSpec((1, 32, CONV_CH), lambda b, i: (b, 0, 0))),
        scratch_shapes=[pltpu.VMEM((32 + t + SUBLANES, CONV_CH), F32),
                        pltpu.VMEM((SUBLANES, 32 + t, CONV_CH), F32),
                        pltpu.VMEM((t, CONV_CH), F32)],
        compiler_params=_cparams(("parallel", "arbitrary")),
        name="conv_prompt",
    )(z, *params)
    return y, nb[:, 32 - (CONV_K - 1):]


def _conv_sample_kernel(z_ref, buf_ref, dw_ref, dwb_ref, lng_ref, lnb_ref, c_ref, u_ref):
    z = z_ref[...]
    u = z[:, :CONV_CH] * _sigmoid(z[:, CONV_CH:])
    u_ref[...] = u
    acc = dwb_ref[...] + dw_ref[CONV_K - 1:CONV_K, :] * u
    for j in range(CONV_K - 1):
        acc = acc + dw_ref[j:j + 1, :] * buf_ref[:, j, :]
    c = _layer_norm(acc, lng_ref[...], lnb_ref[...])
    c_ref[...] = c * _sigmoid(c)


def _conv_sample(z_c, buf, prm):
    B = z_c.shape[0]
    row = lambda x: x.reshape(1, -1)
    args = [z_c, buf, prm["conv_dw"], row(prm["conv_dw_b"]), row(prm["conv_ln_g"]), row(prm["conv_ln_b"])]
    return pl.pallas_call(
        _conv_sample_kernel,
        out_shape=(jax.ShapeDtypeStruct((B, CONV_CH), F32), jax.ShapeDtypeStruct((B, CONV_CH), F32)),
        compiler_params=pltpu.CompilerParams(vmem_limit_bytes=VMEM_LIMIT),
        name="conv_sample",
    )(*args)


def _attn_group(q_ref, k_ref, v_ref, o_ref, m_scr, l_scr, acc_scr, *, seq, dil, nk, first, last):
    nb = (seq // dil) // Q_BLOCK
    qi = _iota((Q_BLOCK, 2 * Q_BLOCK), 0)
    kj = _iota((Q_BLOCK, 2 * Q_BLOCK), 1)
    dist = qi + Q_BLOCK - kj
    in_win = (dist >= 0) & (dist <= nk)
    is_cur = kj >= Q_BLOCK
    lane_lo = _iota((Q_BLOCK, LANES), 1) < HEAD_DIM

    def rows(start):
        return pl.ds(start, Q_BLOCK) if dil == 1 else pl.ds(start, Q_BLOCK, stride=dil)

    def body(it, carry):
        blocks = []
        for u in range(ATTN_UNROLL):
            blk = it * ATTN_UNROLL + u
            d = blk // nb
            n = blk - d * nb
            q0 = d + n * (Q_BLOCK * dil)
            p0 = jnp.maximum(q0 - Q_BLOCK * dil, d)
            blocks.append((q0, p0, in_win & (is_cur | (n > 0))))
        qb = [q_ref[0, rows(q0), :] * ATTN_SCALE for q0, _, _ in blocks]
        kb = [jnp.concatenate([k_ref[0, rows(p0), :], k_ref[0, rows(q0), :]], axis=0).astype(BF16)
              for q0, p0, _ in blocks]
        vb = [jnp.concatenate([v_ref[0, rows(p0), :], v_ref[0, rows(q0), :]], axis=0).astype(BF16)
              for q0, p0, _ in blocks]
        units = [(u, lo) for u in range(ATTN_UNROLL) for lo in (True, False)]
        s = [_dot_nt(jnp.where(lane_lo == lo, qb[u], 0.0), kb[u]) for u, lo in units]
        s = [jnp.where(blocks[u][2], x, NEG_BIG) for (u, _), x in zip(units, s)]
        m = [jnp.max(x, axis=1, keepdims=True) for x in s]
        p = [jnp.exp(x - mm) for x, mm in zip(s, m)]
        l = [jnp.sum(x, axis=1, keepdims=True) for x in p]
        o = [jnp.dot(x.astype(BF16), vb[u], preferred_element_type=F32) for (u, _), x in zip(units, p)]
        for u in range(ATTN_UNROLL):
            m_b = jnp.where(lane_lo, m[2 * u], m[2 * u + 1])
            l_b = jnp.where(lane_lo, l[2 * u], l[2 * u + 1])
            o_b = jnp.where(lane_lo, o[2 * u], o[2 * u + 1])
            r = rows(blocks[u][0])
            if not first:
                m_old = m_scr[r, :]
                m_new = jnp.maximum(m_old, m_b)
                a_old = jnp.exp(m_old - m_new)
                a_new = jnp.exp(m_b - m_new)
                l_b = l_scr[r, :] * a_old + l_b * a_new
                o_b = acc_scr[r, :] * a_old + o_b * a_new
                m_b = m_new
            if last:
                o_ref[0, r, :] = o_b / l_b
            else:
                m_scr[r, :] = m_b
                l_scr[r, :] = l_b
                acc_scr[r, :] = o_b
        return carry

    lax.fori_loop(0, seq // (Q_BLOCK * ATTN_UNROLL), body, 0)


def _attn_prompt_kernel(q_ref, k_ref, v_ref, o_ref, m_scr, l_scr, acc_scr, *, seq):
    g = pl.program_id(2)
    for gi, (win, dil) in enumerate(SWA_GROUPS):
        @pl.when(g == gi)
        def _(dil=dil, nk=win // dil, gi=gi):
            _attn_group(q_ref, k_ref, v_ref, o_ref, m_scr, l_scr, acc_scr, seq=seq, dil=dil, nk=nk,
                        first=gi == 0, last=gi == N_GROUPS - 1)


def _attn_prompt(z):
    B, S, _ = z.shape
    assert Z_QKV0 % LANES == 0
    for win, dil in SWA_GROUPS:
        assert S % (dil * Q_BLOCK) == 0 and win // dil <= Q_BLOCK
    assert (S // Q_BLOCK) % ATTN_UNROLL == 0
    pairs = G_WIDTH // LANES
    blk = lambda base: pl.BlockSpec((1, S, LANES), lambda b, p, g: (b, 0, Z_QKV0 // LANES + base + g * pairs + p))
    return pl.pallas_call(
        functools.partial(_attn_prompt_kernel, seq=S),
        out_shape=jax.ShapeDtypeStruct((B, S, G_WIDTH), F32),
        grid=(B, pairs, N_GROUPS),
        in_specs=[blk(0), blk(A_WIDTH // LANES), blk(2 * A_WIDTH // LANES)],
        out_specs=pl.BlockSpec((1, S, LANES), lambda b, p, g: (b, 0, p)),
        scratch_shapes=[pltpu.VMEM((S, LANES), F32)] * 3,
        compiler_params=_cparams(("parallel", "parallel", "arbitrary")),
        name="attn_prompt",
    )(z, z, z)


def _attn_sample_kernel(q_ref, kn_ref, vn_ref, ca_ref, cb_ref, cc_ref, o_ref, *, bs):
    caches = (ca_ref, cb_ref, cc_ref)
    hrow = _iota((SUBLANES, G_WIDTH), 0)
    hmask = hrow == (_iota((SUBLANES, G_WIDTH), 1) >> 6)
    outs = []
    for n in range(bs):
        m_run = l_run = acc = None
        for gi, (win, dil) in enumerate(SWA_GROUPS):
            gs = slice(gi * G_WIDTH, (gi + 1) * G_WIDTH)
            q = q_ref[0, n:n + 1, gs] * ATTN_SCALE
            qr = jnp.where(hmask, q, 0.0).astype(BF16)
            k_t = caches[gi][0, n, 0].reshape(G_WIDTH, win).astype(BF16)
            v_t = caches[gi][0, n, 1].reshape(G_WIDTH, win).astype(BF16)
            kn = kn_ref[0, n:n + 1, gs].astype(BF16).astype(F32)
            vn = vn_ref[0, n:n + 1, gs].astype(BF16).astype(F32)
            valid = (_iota((SUBLANES, win), 1) & (dil - 1)) == 0
            s = jnp.where(valid, jnp.dot(qr, k_t, preferred_element_type=F32), NEG_BIG)
            s_n = jnp.sum(qr.astype(F32) * kn, axis=1, keepdims=True)
            m = jnp.maximum(jnp.max(s, axis=1, keepdims=True), s_n)
            p = jnp.exp(s - m)
            p_n = jnp.exp(s_n - m)
            l = jnp.sum(p, axis=1, keepdims=True) + p_n
            o = _dot_nt(p, v_t) + p_n.astype(BF16).astype(F32) * vn
            if m_run is None:
                m_run, l_run, acc = m, l, o
            else:
                m_new = jnp.maximum(m_run, m)
                a_old = jnp.exp(m_run - m_new)
                a_new = jnp.exp(m - m_new)
                l_run = l_run * a_old + l * a_new
                acc = acc * a_old + o * a_new
                m_run = m_new
        outs.append(jnp.sum(jnp.where(hmask, acc / l_run, 0.0), axis=0, keepdims=True))
    o_ref[0] = jnp.concatenate(outs, axis=0)


def _attn_sample(q, k_new, v_new, caches_t, layer, *, bs=2):
    B = q.shape[0]
    assert B % bs == 0
    for (win, dil), c in zip(SWA_GROUPS, caches_t):
        assert c.shape[-1] == win and win % dil == 0 and dil & (dil - 1) == 0
    vec = pl.BlockSpec((1, bs, A_WIDTH), lambda i: (i, 0, 0))
    cache_spec = lambda c: pl.BlockSpec((1, bs) + c.shape[2:], lambda i: (layer, i, 0, 0, 0, 0))
    rows = lambda x: x.reshape(B // bs, bs, A_WIDTH)
    out = pl.pallas_call(
        functools.partial(_attn_sample_kernel, bs=bs),
        out_shape=jax.ShapeDtypeStruct((B // bs, bs, G_WIDTH), F32),
        grid=(B // bs,),
        in_specs=[vec, vec, vec] + [cache_spec(c) for c in caches_t],
        out_specs=pl.BlockSpec((1, bs, G_WIDTH), lambda i: (i, 0, 0)),
        compiler_params=_cparams(("parallel",)),
        name="attn_sample",
    )(rows(q), rows(k_new), rows(v_new), *caches_t)
    return out.reshape(B, G_WIDTH)


def _cache_shift_kernel(new_ref, ca_ref, cb_ref, cc_ref, oa_ref, ob_ref, oc_ref, *, bs):
    rows = 2 * G_WIDTH
    for gi, (c_ref, o_ref) in enumerate(((ca_ref, oa_ref), (cb_ref, ob_ref), (cc_ref, oc_ref))):
        win = c_ref.shape[-1]
        last = _iota((rows, win), 1) == win - 1
        for n in range(bs):
            x = c_ref[0, n].reshape(rows, win)
            col = new_ref[0, gi * rows:(gi + 1) * rows, n:n + 1]
            o_ref[0, n] = jnp.where(last, col, pltpu.roll(x, win - 1, 1)).reshape(o_ref.shape[2:])


def _cache_shift(caches_t, new_cols, *, bs=2):
    depth, B = caches_t[0].shape[:2]
    assert B % bs == 0
    nb = B // bs
    spec = lambda c: pl.BlockSpec((1, bs) + c.shape[2:], lambda l, i: (l, i, 0, 0, 0, 0))
    n_rows = new_cols.shape[1]
    new_b = new_cols.reshape(depth, n_rows, nb, bs).transpose(0, 2, 1, 3).reshape(depth * nb, n_rows, bs)
    return pl.pallas_call(
        functools.partial(_cache_shift_kernel, bs=bs),
        out_shape=tuple(jax.ShapeDtypeStruct(c.shape, c.dtype) for c in caches_t),
        grid=(depth, nb),
        in_specs=[pl.BlockSpec((1, n_rows, bs), lambda l, i: (l * nb + i, 0, 0))] + [spec(c) for c in caches_t],
        out_specs=tuple(spec(c) for c in caches_t),
        compiler_params=_cparams(("parallel", "parallel")),
        name="cache_shift",
    )(new_b, *caches_t)


def _merge_kernel(gt_ref, yr_ref, yc_ref, ao_ref, x_ref, wa_ref, wo_ref, g_ref, b_ref, h_ref, *, alpha):
    d = x_ref.shape[1]
    gate = lambda i: gt_ref[:, i * d:(i + 1) * d].astype(F32)
    y_a = _dot(ao_ref[...], wa_ref[...])
    merged = gate(0) * yr_ref[...].astype(F32) + gate(1) * yc_ref[...].astype(F32) + gate(2) * y_a
    h_ref[...] = _layer_norm(alpha * x_ref[...] + _dot(merged, wo_ref[...]), g_ref[...], b_ref[...])


def _merge(gates, y_r, y_c, a_o, x, prm, *, alpha, t=512):
    M, D = x.shape
    t = min(t, M)
    assert M % t == 0
    row = lambda a: a.reshape(1, -1)
    tile = lambda a: pl.BlockSpec((t, a.shape[1]), lambda i: (i, 0))
    full = lambda a: pl.BlockSpec(a.shape, lambda i: (0, 0))
    acts = [gates, y_r, y_c, a_o, x]
    params = [prm["w_attn_out_bf"], prm["w_o_bf"], row(prm["ln1_g"]), row(prm["ln1_b"])]
    return pl.pallas_call(
        functools.partial(_merge_kernel, alpha=alpha),
        out_shape=jax.ShapeDtypeStruct((M, D), F32),
        grid=(M // t,),
        in_specs=[tile(a) for a in acts] + [full(a) for a in params],
        out_specs=pl.BlockSpec((t, D), lambda i: (i, 0)),
        compiler_params=_cparams(("parallel",)),
        name="merge",
    )(*acts, *params)


def _ffn_kernel(*refs, alpha, t, tiles_per_seq, sample):
    if sample:
        (h_ref, pg_ref, pu_ref, wg_ref, wu_ref, dg_ref, du_ref, bg_ref, bu_ref, wo_ref, g_ref, b_ref,
         o_ref, ng_ref, nu_ref, acc_scr) = refs
    else:
        (h_ref, wg_ref, wu_ref, dg_ref, du_ref, bg_ref, bu_ref, wo_ref, g_ref, b_ref,
         o_ref, ng_ref, nu_ref, acc_scr, cg_scr, cu_scr) = refs
    i = pl.program_id(0)
    j = pl.program_id(1)
    nj = pl.num_programs(1)

    @pl.when(j == 0)
    def _():
        acc_scr[...] = jnp.zeros_like(acc_scr)

    def conv(u, dw_ref, bias_ref, p1, p2):
        return dw_ref[0:1, :] * p2 + dw_ref[1:2, :] * p1 + dw_ref[2:3, :] * u + bias_ref[...]

    if sample:
        hb = h_ref[...].astype(BF16)
        y = []
        for w_ref, dw_ref, bias_ref, new_ref, prev_ref in ((wg_ref, dg_ref, bg_ref, ng_ref, pg_ref),
                                                           (wu_ref, du_ref, bu_ref, nu_ref, pu_ref)):
            u = jnp.dot(hb, w_ref[...], preferred_element_type=F32)
            new_ref[...] = u
            y.append(conv(u, dw_ref, bias_ref, prev_ref[1], prev_ref[0]))
        acc_scr[...] += _dot((y[0] * _sigmoid(y[0])) * y[1], wo_ref[...])
    else:
        tp = t // FFN_ROW_PARTS
        fresh = (i % tiles_per_seq) == 0
        row = _iota((SUBLANES, wg_ref.shape[1]), 0)
        hb = [h_ref[s * tp:(s + 1) * tp, :].astype(BF16) for s in range(FFN_ROW_PARTS)]
        branches = ((wg_ref, dg_ref, bg_ref, ng_ref, cg_scr), (wu_ref, du_ref, bu_ref, nu_ref, cu_scr))
        u = [[jnp.dot(hb[s], w_ref[...], preferred_element_type=F32) for w_ref, *_ in branches]
             for s in range(FFN_ROW_PARTS)]
        for s in range(FFN_ROW_PARTS):
            y = []
            for bi, (_, dw_ref, bias_ref, new_ref, c_scr) in enumerate(branches):
                us = u[s][bi]
                if s == 0:
                    c1 = jnp.where(fresh, 0.0, c_scr[j, 7:8, :])
                    c2 = jnp.where(fresh, 0.0, c_scr[j, 6:7, :])
                else:
                    c1 = u[s - 1][bi][tp - 1:tp, :]
                    c2 = u[s - 1][bi][tp - 2:tp - 1, :]
                r1 = pltpu.roll(us, 1, 0)
                r2 = pltpu.roll(us, 2, 0)
                p1 = jnp.concatenate([jnp.where(row == 0, c1, r1[:SUBLANES]), r1[SUBLANES:]], axis=0)
                p2 = jnp.concatenate([jnp.where(row == 0, c2, jnp.where(row == 1, c1, r2[:SUBLANES])),
                                      r2[SUBLANES:]], axis=0)
                y.append(conv(us, dw_ref, bias_ref, p1, p2))
                if s == FFN_ROW_PARTS - 1:
                    new_ref[0] = us[tp - 2:tp, :]
                    c_scr[j, :, :] = us[tp - SUBLANES:tp, :]
            acc_scr[s * tp:(s + 1) * tp, :] += _dot((y[0] * _sigmoid(y[0])) * y[1], wo_ref[...])

    @pl.when(j == nj - 1)
    def _():
        o_ref[...] = _layer_norm(alpha * h_ref[...] + acc_scr[...], g_ref[...], b_ref[...])


def _ffn(h, prm, *, alpha, seq_len, prev=None, t=512, tn=2816):
    M, D = h.shape
    d_ff = prm["w_ffn_out_bf"].shape[0]
    sample = prev is not None
    t = M if sample else min(t, seq_len)
    tn = min(tn, d_ff)
    assert M % t == 0 and d_ff % tn == 0 and (sample or (seq_len % t == 0 and t >= 8))
    J = d_ff // tn
    row = lambda a: a.reshape(1, -1)
    w_in, dw, dwb = prm["w_ffn_in_bf"], prm["ffn_dw"], row(prm["ffn_dw_b"])
    col_g = lambda shape: pl.BlockSpec(shape, lambda i, j: (0, j))
    col_u = lambda shape: pl.BlockSpec(shape, lambda i, j: (0, J + j))
    const = lambda a: pl.BlockSpec(a.shape, lambda i, j: (0, 0))
    ins, specs = [h], [pl.BlockSpec((t, D), lambda i, j: (i, 0))]
    if sample:
        pstack = jnp.stack(prev)
        ins += [pstack, pstack]
        specs += [pl.BlockSpec((2, M, tn), lambda i, j: (0, 0, j)), pl.BlockSpec((2, M, tn), lambda i, j: (0, 0, J + j))]
    ins += [w_in, w_in, dw, dw, dwb, dwb, prm["w_ffn_out_bf"], row(prm["ln2_g"]), row(prm["ln2_b"])]
    specs += [col_g((D, tn)), col_u((D, tn)), col_g((FFN_K, tn)), col_u((FFN_K, tn)), col_g((1, tn)),
              col_u((1, tn)), pl.BlockSpec((tn, D), lambda i, j: (j, 0)), const(row(prm["ln2_g"])),
              const(row(prm["ln2_b"]))]
    if sample:
        new_shape = jax.ShapeDtypeStruct((M, d_ff), F32)
        new_spec = pl.BlockSpec((M, tn), lambda i, j: (0, j))
        scratch = [pltpu.VMEM((t, D), F32)]
        tiles_per_seq = 1
    else:
        tiles_per_seq = seq_len // t
        new_shape = jax.ShapeDtypeStruct((M // t, 2, d_ff), F32)
        new_spec = pl.BlockSpec((1, 2, tn), lambda i, j: (i, 0, j))
        scratch = [pltpu.VMEM((t, D), F32), pltpu.VMEM((J, 8, tn), F32), pltpu.VMEM((J, 8, tn), F32)]
    out, new_g, new_u = pl.pallas_call(
        functools.partial(_ffn_kernel, alpha=alpha, t=t, tiles_per_seq=tiles_per_seq, sample=sample),
        out_shape=(jax.ShapeDtypeStruct((M, D), F32), new_shape, new_shape),
        grid=(M // t, J),
        in_specs=specs,
        out_specs=(pl.BlockSpec((t, D), lambda i, j: (i, 0)), new_spec, new_spec),
        scratch_shapes=scratch,
        compiler_params=_cparams(("arbitrary", "arbitrary")),
        name="ffn_sample" if sample else "ffn_prompt",
    )(*ins)
    if not sample:
        new_g = new_g[tiles_per_seq - 1::tiles_per_seq]
        new_u = new_u[tiles_per_seq - 1::tiles_per_seq]
    return out, new_g, new_u


def _layer_weights(l, w_in, w_rwkv_out, w_conv_out, w_attn_out, w_o, w_ffn_in, w_ffn_out, small):
    prm = {k: v[l] for k, v in small.items()}
    wi = w_in[l]
    o_c = R_COLS
    o_q = o_c + 2 * CONV_CH
    o_g = o_q + 3 * A_WIDTH
    prm["w_in_bf"] = jnp.concatenate([wi[:, o_g:], wi[:, :o_c], wi[:, o_q:o_g], wi[:, o_c:o_q]], axis=1).astype(BF16)
    prm["w_rwkv_out_bf"] = w_rwkv_out[l].astype(BF16)
    prm["w_conv_out_bf"] = w_conv_out[l].astype(BF16)
    prm["w_attn_out_bf"] = w_attn_out[l].astype(BF16)
    prm["w_o_bf"] = w_o[l].astype(BF16)
    prm["w_ffn_in_bf"] = w_ffn_in[l].astype(BF16)
    prm["w_ffn_out_bf"] = w_ffn_out[l].astype(BF16)
    return prm


def _project_kernel(x_ref, w_ref, gb_ref, g_ref, z_ref, *, n_gate):
    j = pl.program_id(1)
    xw = lambda: jnp.dot(x_ref[...].astype(BF16), w_ref[...], preferred_element_type=F32)

    @pl.when(j < n_gate)
    def _():
        g_ref[...] = _sigmoid(xw() + gb_ref[...]).astype(g_ref.dtype)

    @pl.when(j >= n_gate)
    def _():
        z_ref[...] = xw()


def _project(x2, prm, tm, tn=1024):
    M, K = x2.shape
    w = prm["w_in_bf"]
    gate_w = prm["gate_b"].size
    tm = min(tm, M)
    assert M % tm == 0 and gate_w % tn == 0 and Z_WIDTH % tn == 0 and w.shape[1] == gate_w + Z_WIDTH
    n_gate = gate_w // tn
    return pl.pallas_call(
        functools.partial(_project_kernel, n_gate=n_gate),
        out_shape=(jax.ShapeDtypeStruct((M, gate_w), BF16), jax.ShapeDtypeStruct((M, Z_WIDTH), F32)),
        grid=(M // tm, w.shape[1] // tn),
        in_specs=[pl.BlockSpec((tm, K), lambda i, j: (i, 0)),
                  pl.BlockSpec((K, tn), lambda i, j: (0, j)),
                  pl.BlockSpec((1, tn), lambda i, j: (0, jnp.minimum(j, n_gate - 1)))],
        out_specs=(pl.BlockSpec((tm, tn), lambda i, j: (i, jnp.minimum(j, n_gate - 1))),
                   pl.BlockSpec((tm, tn), lambda i, j: (i, jnp.maximum(j - n_gate, 0)))),
        compiler_params=_cparams(("parallel", "arbitrary")),
        name="project",
    )(x2, w, prm["gate_b"].reshape(1, -1))


def _kv_rows(z_qkv, gi):
    k = z_qkv[..., A_WIDTH + gi * G_WIDTH:A_WIDTH + (gi + 1) * G_WIDTH]
    v = z_qkv[..., 2 * A_WIDTH + gi * G_WIDTH:2 * A_WIDTH + (gi + 1) * G_WIDTH]
    kv = jnp.stack([k, v], axis=-2)
    return kv.reshape(kv.shape[:-1] + (G_HEADS, HEAD_DIM))


def _prompt_layer(x, prm, alpha):
    B, L, D = x.shape
    x2 = x.reshape(B * L, D)
    gates, z = _project(x2, prm, tm=1024)
    z = z.reshape(B, L, Z_WIDTH)
    y_r, new_wkv = _rwkv_prompt(z, prm)
    y_c, new_conv = _conv_prompt(z, prm)
    a_o = _attn_prompt(z)
    h = _merge(gates, y_r.reshape(B * L, D), y_c.reshape(B * L, D), a_o.reshape(B * L, -1), x2, prm, alpha=alpha)
    out, nf_g, nf_u = _ffn(h, prm, alpha=alpha, seq_len=L)
    new_kv = [_kv_rows(z[:, L - min(win, L):, Z_QKV0:Z_C0], gi) for gi, (win, _) in enumerate(SWA_GROUPS)]
    return (out.reshape(B, L, D), z[:, L - 1:, :R_COLS], new_wkv, new_conv, new_kv,
            jnp.concatenate([nf_g, nf_u], axis=-1))


def _sample_layer(x, prm, alpha, shift, wkv, conv_buf, ffn_buf, caches_t, layer):
    B, L, D = x.shape
    assert L == 1
    x2 = x.reshape(B, D)
    z_g, z = _project(x2, prm, tm=B)
    z_r, z_qkv, z_c = z[:, :Z_QKV0], z[:, Z_QKV0:Z_C0], z[:, Z_C0:]
    o_r, new_wkv = _rwkv_sample(z_r, shift.reshape(B, R_COLS), wkv, prm)
    y_r = _mm(o_r, prm["w_rwkv_out_bf"], tm=B, tn=1024, out_dtype=BF16)
    c, u = _conv_sample(z_c, conv_buf, prm)
    y_c = _mm(c, prm["w_conv_out_bf"], tm=B, tn=1024, out_dtype=BF16)
    a_o = _attn_sample(z_qkv[:, :A_WIDTH], z_qkv[:, A_WIDTH:2 * A_WIDTH], z_qkv[:, 2 * A_WIDTH:], caches_t, layer)
    h = _merge(z_g, y_r, y_c, a_o, x2, prm, alpha=alpha)
    out, u_g, u_u = _ffn(h, prm, alpha=alpha, seq_len=1, prev=(ffn_buf[:, 0], ffn_buf[:, 1]))
    kv_new = []
    for gi in range(N_GROUPS):
        kv_new += [z_qkv[:, A_WIDTH + gi * G_WIDTH:A_WIDTH + (gi + 1) * G_WIDTH],
                   z_qkv[:, 2 * A_WIDTH + gi * G_WIDTH:2 * A_WIDTH + (gi + 1) * G_WIDTH]]
    return (out.reshape(B, 1, D), z_r.reshape(B, 1, R_COLS), new_wkv, u, jnp.concatenate(kv_new, axis=-1),
            jnp.concatenate([u_g, u_u], axis=-1))


def _append_row(buf, rows):
    return jnp.concatenate([buf[:, :, 1:], rows[:, :, None]], axis=2)


def kernel(x_prompt, x_sample, state_shift, state_wkv, state_conv, cache_swa_a, cache_swa_b, cache_swa_c, state_ffn, w_in, rwkv_mu, rwkv_w0, rwkv_w_up, rwkv_a0, rwkv_a_up, rwkv_g_up, rwkv_k_k, rwkv_k_a, rwkv_r_k, rwkv_ln_g, rwkv_ln_b, w_rwkv_out, conv_dw, conv_dw_b, conv_ln_g, conv_ln_b, w_conv_out, w_attn_out, gate_b, w_o, ln1_g, ln1_b, w_ffn_in, ffn_dw, ffn_dw_b, w_ffn_out, ln2_g, ln2_b):
    depth = w_in.shape[0]
    alpha = (2 * depth) ** 0.25
    small = dict(rwkv_mu=rwkv_mu, rwkv_w0=rwkv_w0, rwkv_w_up=rwkv_w_up, rwkv_a0=rwkv_a0, rwkv_a_up=rwkv_a_up,
                 rwkv_g_up=rwkv_g_up, rwkv_k_k=rwkv_k_k, rwkv_k_a=rwkv_k_a,
                 rwkv_r_k=rwkv_r_k.reshape(depth, -1), rwkv_ln_g=rwkv_ln_g, rwkv_ln_b=rwkv_ln_b,
                 conv_dw=conv_dw, conv_dw_b=conv_dw_b, conv_ln_g=conv_ln_g, conv_ln_b=conv_ln_b,
                 gate_b=gate_b.reshape(depth, -1), ln1_g=ln1_g, ln1_b=ln1_b, ffn_dw=ffn_dw, ffn_dw_b=ffn_dw_b,
                 ln2_g=ln2_g, ln2_b=ln2_b)
    caches = (cache_swa_a, cache_swa_b, cache_swa_c)
    caches_t = [jnp.transpose(c, (0, 1, 3, 4, 5, 2)) for c in caches]
    hp, hs = x_prompt, x_sample
    outs_p, outs_s = [], []
    for l in range(depth):
        prm = _layer_weights(l, w_in, w_rwkv_out, w_conv_out, w_attn_out, w_o, w_ffn_in, w_ffn_out, small)
        res_p = _prompt_layer(hp, prm, alpha)
        hp = res_p[0]
        outs_p.append(res_p[1:])
        res_s = _sample_layer(hs, prm, alpha, state_shift[l], state_wkv[l], state_conv[l], state_ffn[l],
                              caches_t, l)
        hs = res_s[0]
        outs_s.append(res_s[1:])
    stk = lambda outs, f: jnp.stack([f(o) for o in outs])
    res = [hp, hs]
    res += [stk(outs_p, lambda o: o[0]), stk(outs_s, lambda o: o[0])]
    res += [stk(outs_p, lambda o: o[1]), stk(outs_s, lambda o: o[1])]
    res += [stk(outs_p, lambda o: o[2]), _append_row(state_conv, stk(outs_s, lambda o: o[2]))]
    new_cols = jnp.transpose(stk(outs_s, lambda o: o[3]), (0, 2, 1))
    shifted = _cache_shift(caches_t, new_cols)
    for gi in range(N_GROUPS):
        res += [stk(outs_p, lambda o: o[3][gi]), jnp.transpose(shifted[gi], (0, 1, 5, 2, 3, 4))]
    res += [stk(outs_p, lambda o: o[4]), _append_row(state_ffn, stk(outs_s, lambda o: o[4]))]
    return tuple(res)
```

```python
import functools

import jax
import jax.numpy as jnp
from jax import lax
from jax.experimental import pallas as pl
from jax.experimental.pallas import tpu as pltpu

F32 = jnp.float32
BF16 = jnp.bfloat16

HEAD_DIM = 64
R_HEADS = 8
R_WIDTH = R_HEADS * HEAD_DIM
DECAY_LORA = 64
ICLR_LORA = 64
GATE_LORA = 128
R_COLS = 3 * R_WIDTH + DECAY_LORA + ICLR_LORA + GATE_LORA
LNX_EPS = 64e-5
CONV_CH = 512
CONV_K = 31
SWA_GROUPS = ((128, 1), (512, 4), (2048, 16))
N_GROUPS = len(SWA_GROUPS)
G_HEADS = 4
G_WIDTH = G_HEADS * HEAD_DIM
A_WIDTH = N_GROUPS * G_WIDTH
Q_BLOCK = 128
ATTN_SCALE = HEAD_DIM ** -0.5
N_BRANCH = 3
FFN_K = 3
LN_EPS = 1e-5
NEG_BIG = -1e30

Z_QKV0 = R_COLS
Z_C0 = Z_QKV0 + 3 * A_WIDTH
Z_WIDTH = Z_C0 + 2 * CONV_CH

LANES = 128
CHUNK = 64
SUBLANES = 8
CONV_ROWS = 32
SEG_BLOCK = 256
FFN_ROW_PARTS = 2
ATTN_UNROLL = 8
VMEM_LIMIT = 56 * 1024 * 1024


def _cparams(sem):
    return pltpu.CompilerParams(dimension_semantics=sem, vmem_limit_bytes=VMEM_LIMIT)


def _dot(a, b):
    return jnp.dot(a.astype(BF16), b.astype(BF16), preferred_element_type=F32)


def _dot_nt(a, b):
    return lax.dot_general(a.astype(BF16), b.astype(BF16), (((1,), (1,)), ((), ())),
                           preferred_element_type=F32)


def _dot_tn(a, b):
    return lax.dot_general(a.astype(BF16), b.astype(BF16), (((0,), (0,)), ((), ())),
                           preferred_element_type=F32)


def _split3(x):
    h1 = x.astype(BF16)
    r1 = x - h1.astype(F32)
    h2 = r1.astype(BF16)
    h3 = (r1 - h2.astype(F32)).astype(BF16)
    return h1, h2, h3


def _sigmoid(x):
    return 1.0 / (1.0 + jnp.exp(-x))


def _layer_norm(x, g, b):
    mu = jnp.mean(x, axis=-1, keepdims=True)
    xc = x - mu
    var = jnp.mean(xc * xc, axis=-1, keepdims=True)
    return xc * lax.rsqrt(var + LN_EPS) * g + b


def _iota(shape, dim):
    return lax.broadcasted_iota(jnp.int32, shape, dim)


def _head_ones():
    r = _iota((SEG_BLOCK, SEG_BLOCK), 0)
    c = _iota((SEG_BLOCK, SEG_BLOCK), 1)
    return jnp.where((r >> 6) == (c >> 6), 1.0, 0.0).astype(BF16)


def _segsum(x, ones):
    outs = [_dot(x[:, p:p + SEG_BLOCK], ones) for p in range(0, x.shape[1], SEG_BLOCK)]
    return outs[0] if len(outs) == 1 else jnp.concatenate(outs, axis=1)


def _mm_kernel(x_ref, w_ref, o_ref):
    o_ref[...] = jnp.dot(x_ref[...].astype(BF16), w_ref[...],
                         preferred_element_type=F32).astype(o_ref.dtype)


def _mm(x, w, *, tm, tn, out_dtype=F32):
    M, K = x.shape
    N = w.shape[1]
    tm = min(tm, M)
    tn = min(tn, N)
    assert M % tm == 0 and N % tn == 0
    return pl.pallas_call(
        _mm_kernel,
        out_shape=jax.ShapeDtypeStruct((M, N), out_dtype),
        grid=(M // tm, N // tn),
        in_specs=[pl.BlockSpec((tm, K), lambda i, j: (i, 0)),
                  pl.BlockSpec((K, tn), lambda i, j: (0, j))],
        out_specs=pl.BlockSpec((tm, tn), lambda i, j: (i, j)),
        compiler_params=_cparams(("parallel", "parallel")),
        name="mm",
    )(x, w)


def _rwkv_prep(zr, prevs, mu, w0, w_up, a0, a_up, g_up, k_k, k_a, ones):
    zs = zr + (prevs - zr) * mu
    r = zs[:, 0:R_WIDTH]
    k = zs[:, R_WIDTH:2 * R_WIDTH]
    v = zs[:, 2 * R_WIDTH:3 * R_WIDTH]
    o1 = 3 * R_WIDTH
    wd = zs[:, o1:o1 + DECAY_LORA]
    ad = zs[:, o1 + DECAY_LORA:o1 + DECAY_LORA + ICLR_LORA]
    gd = zs[:, o1 + DECAY_LORA + ICLR_LORA:R_COLS]
    nx = -(w0 + _dot(jnp.tanh(wd), w_up))
    softplus = jnp.maximum(nx, 0.0) + jnp.log(1.0 + jnp.exp(-jnp.abs(nx)))
    logdec = -jnp.exp(-softplus - 0.5)
    a = _sigmoid(a0 + _dot(ad, a_up))
    g = _dot(_sigmoid(gd), g_up)
    kk = k * k_k
    kk = kk / jnp.maximum(jnp.sqrt(_segsum(kk * kk, ones)), 1e-12)
    k2 = k * (1.0 + (a - 1.0) * k_a)
    return r, k2, v, kk, a, g, logdec


def _rwkv_post(o, r, k2, v, g, r_k, ln_g, ln_b, ones):
    inv = 1.0 / HEAD_DIM
    m = _segsum(o, ones) * inv
    oc = o - m
    var = _segsum(oc * oc, ones) * inv
    on = oc * lax.rsqrt(var + LNX_EPS) * ln_g + ln_b
    bonus = _segsum(r * k2 * r_k, ones) * v
    return (on + bonus) * g


def _chunk_cumsum(logdec, tri_bd):
    h1, h2, h3 = _split3(logdec)
    d = lambda x: jnp.dot(tri_bd, x, preferred_element_type=F32)
    return d(h1) + d(h2) + d(h3)


def _rwkv_chunk_tables(logdec, cum_all, r, k2, v, kk, a, c):
    sl = slice(c * CHUNK, (c + 1) * CHUNK)
    cum = cum_all[sl]
    ld = logdec[sl]
    cum_end = cum[CHUNK - 1:CHUNK, :]
    p_in = jnp.exp(cum)
    p_ex = jnp.exp(cum - ld)
    p_inv = jnp.exp(-cum)
    p_tail = jnp.exp(cum_end - cum)
    kka = kk[sl] * a[sl]
    return dict(rt=r[sl] * p_in, at=-kk[sl] * p_ex, bt=kka * p_inv, kt=k2[sl] * p_inv,
                bh=kka * p_tail, kh=k2[sl] * p_tail, v=v[sl], pc=jnp.exp(cum_end))


def _rwkv_units(tables, n_pairs, masks):
    strict, incl, eye, lane_lo, row_lo, lane_lo_n = masks
    n = 2 * CHUNK
    units = [(tb, slice(p * LANES, (p + 1) * LANES)) for tb in tables for p in range(n_pairs)]
    swap = lambda x: pltpu.roll(x, HEAD_DIM, 1)
    masked = lambda x: jnp.concatenate([jnp.where(lane_lo, x, 0.0), jnp.where(lane_lo, 0.0, x)], axis=0)
    local_lo = lambda x: jnp.concatenate([x, swap(x)], axis=0)
    local_hi = lambda x: jnp.concatenate([swap(x), x], axis=0)
    twice = lambda x: jnp.concatenate([x, x], axis=0)

    def spread(x):
        return jnp.where(row_lo, jnp.where(lane_lo_n, x, 0.0), jnp.where(lane_lo_n, 0.0, swap(x)))

    aa, a_lo, v_hi, r_m, bk_m = [], [], [], [], []
    for tb, ls in units:
        r_m.append(masked(tb["rt"][:, ls]))
        aa.append(_dot_nt(jnp.concatenate([masked(tb["at"][:, ls]), r_m[-1]], axis=0),
                          jnp.concatenate([twice(tb["bt"][:, ls]), twice(tb["kt"][:, ls])], axis=0)))
        a_lo.append(local_lo(tb["at"][:, ls]))
        v_hi.append(local_hi(tb["v"][:, ls]))
        bk_m.append(jnp.concatenate([masked(tb["bh"][:, ls]), masked(tb["kh"][:, ls])], axis=0))
    lj = [jnp.where(strict, a[:n, :n], 0.0) for a in aa]
    x = [jnp.where(lane_lo_n, al, _dot(jnp.where(strict, a[:n, n:], 0.0), vh))
         for al, vh, a in zip(a_lo, v_hi, aa)]
    steps = CHUNK.bit_length() - 1
    for j in range(steps):
        if j < steps - 1:
            prod = [_dot(l, jnp.concatenate([l, xx], axis=1)) for l, xx in zip(lj, x)]
            lj = [pr[:, :n] for pr in prod]
            x = [xx + pr[:, n:] for xx, pr in zip(x, prod)]
        else:
            x = [xx + _dot(l, xx) for l, xx in zip(lj, x)]
    rhs = [jnp.concatenate([xx, jnp.where(lane_lo_n, 0.0, vh)], axis=0) for xx, vh in zip(x, v_hi)]
    qo = [_dot(jnp.concatenate([jnp.where(incl, a[n:, :n], 0.0), jnp.where(incl, a[n:, n:], 0.0)], axis=1), rh)
          for a, rh in zip(aa, rhs)]
    gh = [_dot_tn(bk, rh) for bk, rh in zip(bk_m, rhs)]
    return [(rm + spread(q), q, spread(g) + jnp.where(eye, tb["pc"][:, ls], 0.0), g)
            for (tb, ls), rm, q, g in zip(units, r_m, qo, gh)]


def _rwkv_masks():
    n = 2 * CHUNK
    rr = _iota((n, n), 0)
    cc = _iota((n, n), 1)
    same = (rr >> 6) == (cc >> 6)
    strict = same & ((cc & 63) < (rr & 63))
    incl = same & ((cc & 63) <= (rr & 63))
    eye = rr == cc
    lane_lo = _iota((CHUNK, LANES), 1) < HEAD_DIM
    return strict, incl, eye, lane_lo, rr < CHUNK, cc < HEAD_DIM


def _rwkv_prompt_kernel(z_ref, mu_ref, w0_ref, wup_ref, a0_ref, aup_ref, gup_ref, kk_ref, ka_ref,
                        rk_ref, lng_ref, lnb_ref, wout_ref, y_ref, st_ref, prev_scr, st_scr, *, ct):
    i = pl.program_id(1)

    @pl.when(i == 0)
    def _():
        prev_scr[...] = jnp.zeros_like(prev_scr)
        st_scr[...] = jnp.zeros_like(st_scr)

    zr = z_ref[0]
    row = _iota(zr.shape, 0)
    prevs = jnp.where(row == 0, prev_scr[0:1, :], pltpu.roll(zr, 1, 0))
    prev_scr[0:1, :] = zr[ct - 1:ct, :]
    ones = _head_ones()
    r, k2, v, kk, a, g, logdec = _rwkv_prep(
        zr, prevs, mu_ref[...], w0_ref[...], wup_ref[...], a0_ref[...], aup_ref[...], gup_ref[...],
        kk_ref[...], ka_ref[...], ones)

    rr = _iota((ct, ct), 0)
    cc = _iota((ct, ct), 1)
    tri_bd = jnp.where(((rr >> 6) == (cc >> 6)) & (cc <= rr), 1.0, 0.0).astype(BF16)
    masks = _rwkv_masks()
    n_pairs = R_WIDTH // LANES
    cum_all = _chunk_cumsum(logdec, tri_bd)
    n_chunks = ct // CHUNK
    tables = [_rwkv_chunk_tables(logdec, cum_all, r, k2, v, kk, a, c) for c in range(n_chunks)]
    units = _rwkv_units(tables, n_pairs, masks)
    states = [st_scr[p] for p in range(n_pairs)]
    lane_lo, lane_lo_n = masks[3], masks[5]
    o_rows = []
    for c in range(n_chunks):
        cur = units[c * n_pairs:(c + 1) * n_pairs]
        ostk = [_dot(qp, st) + op for (qp, op, _, _), st in zip(cur, states)]
        states = [jnp.where(lane_lo_n, 0.0, _dot(gm, st) + hm) for (_, _, gm, hm), st in zip(cur, states)]
        o_rows.append(jnp.concatenate(
            [jnp.where(lane_lo, pltpu.roll(o[:CHUNK], HEAD_DIM, 1), o[CHUNK:]) for o in ostk], axis=1))
    for p in range(n_pairs):
        st_scr[p] = states[p]
    o = o_rows[0] if len(o_rows) == 1 else jnp.concatenate(o_rows, axis=0)
    out = _rwkv_post(o, r, k2, v, g, rk_ref[...], lng_ref[...], lnb_ref[...], ones)
    y_ref[0] = _dot(out, wout_ref[...]).astype(y_ref.dtype)
    st_ref[0] = st_scr[...]


def _rwkv_prompt(z_r, prm, *, ct=256):
    B, L, _ = z_r.shape
    assert L % ct == 0 and ct % CHUNK == 0
    n_pairs = R_WIDTH // LANES
    row = lambda x: x.reshape(1, -1)
    full = lambda a: pl.BlockSpec(a.shape, lambda b, i: (0,) * a.ndim)
    params = [row(prm["rwkv_mu"]), row(prm["rwkv_w0"]), prm["rwkv_w_up"], row(prm["rwkv_a0"]),
              prm["rwkv_a_up"], prm["rwkv_g_up"], row(prm["rwkv_k_k"]), row(prm["rwkv_k_a"]),
              row(prm["rwkv_r_k"]), row(prm["rwkv_ln_g"]), row(prm["rwkv_ln_b"]), prm["w_rwkv_out_bf"]]
    y, st = pl.pallas_call(
        functools.partial(_rwkv_prompt_kernel, ct=ct),
        out_shape=(jax.ShapeDtypeStruct((B, L, prm["w_rwkv_out_bf"].shape[1]), BF16),
                   jax.ShapeDtypeStruct((B, n_pairs, LANES, LANES), F32)),
        grid=(B, L // ct),
        in_specs=[pl.BlockSpec((1, ct, R_COLS), lambda b, i: (b, i, 0))] + [full(a) for a in params],
        out_specs=(pl.BlockSpec((1, ct, prm["w_rwkv_out_bf"].shape[1]), lambda b, i: (b, i, 0)),
                   pl.BlockSpec((1, n_pairs, LANES, LANES), lambda b, i: (b, 0, 0, 0))),
        scratch_shapes=[pltpu.VMEM((8, R_COLS), F32), pltpu.VMEM((n_pairs, LANES, LANES), F32)],
        compiler_params=_cparams(("parallel", "arbitrary")),
        name="rwkv_prompt",
    )(z_r, *params)
    st = st[..., HEAD_DIM:].reshape(B, R_HEADS, HEAD_DIM, HEAD_DIM)
    return y, jnp.swapaxes(st, -1, -2)


def _rwkv_sample_kernel(z_ref, sh_ref, s_ref, mu_ref, w0_ref, wup_ref, a0_ref, aup_ref, gup_ref,
                        kk_ref, ka_ref, rk_ref, lng_ref, lnb_ref, o_ref, sn_ref):
    ones = _head_ones()
    zr = z_ref[0]
    r, k2, v, kk, a, g, logdec = _rwkv_prep(
        zr, sh_ref[0], mu_ref[...], w0_ref[...], wup_ref[...], a0_ref[...], aup_ref[...],
        gup_ref[...], kk_ref[...], ka_ref[...], ones)
    w = jnp.exp(logdec)
    b = kk * a
    eye = _iota((HEAD_DIM, HEAD_DIM), 0) == _iota((HEAD_DIM, HEAD_DIM), 1)
    col = lambda x: jnp.sum(jnp.where(eye, x, 0.0), axis=1, keepdims=True)
    o_heads = []
    for h in range(R_HEADS):
        hs = slice(h * HEAD_DIM, (h + 1) * HEAD_DIM)
        s = s_ref[0, h]
        sa = jnp.sum(s * (-kk[:, hs]), axis=1, keepdims=True)
        s_new = s * w[:, hs] + sa * b[:, hs] + col(v[:, hs]) * k2[:, hs]
        sn_ref[0, h] = s_new
        o_col = jnp.sum(s_new * r[:, hs], axis=1, keepdims=True)
        o_heads.append(jnp.sum(jnp.where(eye, o_col, 0.0), axis=0, keepdims=True))
    o = jnp.concatenate(o_heads, axis=1)
    o_ref[0] = _rwkv_post(o, r, k2, v, g, rk_ref[...], lng_ref[...], lnb_ref[...], ones)


def _rwkv_sample(z_r, shift, wkv, prm):
    B = z_r.shape[0]
    row = lambda x: x.reshape(1, -1)
    full = lambda a: pl.BlockSpec(a.shape, lambda b: (0,) * a.ndim)
    params = [row(prm["rwkv_mu"]), row(prm["rwkv_w0"]), prm["rwkv_w_up"], row(prm["rwkv_a0"]),
              prm["rwkv_a_up"], prm["rwkv_g_up"], row(prm["rwkv_k_k"]), row(prm["rwkv_k_a"]),
              row(prm["rwkv_r_k"]), row(prm["rwkv_ln_g"]), row(prm["rwkv_ln_b"])]
    o, s_new = pl.pallas_call(
        _rwkv_sample_kernel,
        out_shape=(jax.ShapeDtypeStruct((B, 1, R_WIDTH), F32),
                   jax.ShapeDtypeStruct(wkv.shape, F32)),
        grid=(B,),
        in_specs=[pl.BlockSpec((1, 1, R_COLS), lambda b: (b, 0, 0)),
                  pl.BlockSpec((1, 1, R_COLS), lambda b: (b, 0, 0)),
                  pl.BlockSpec((1,) + wkv.shape[1:], lambda b: (b, 0, 0, 0))] + [full(a) for a in params],
        out_specs=(pl.BlockSpec((1, 1, R_WIDTH), lambda b: (b, 0, 0)),
                   pl.BlockSpec((1,) + wkv.shape[1:], lambda b: (b, 0, 0, 0))),
        compiler_params=_cparams(("parallel",)),
        name="rwkv_sample",
    )(z_r.reshape(B, 1, R_COLS), shift.reshape(B, 1, R_COLS), wkv, *params)
    return o.reshape(B, R_WIDTH), s_new


def _conv_prompt_kernel(z_ref, dw_ref, dwb_ref, lng_ref, lnb_ref, wout_ref, y_ref, nb_ref, u_scr, sh_scr,
                        acc_scr, *, t):
    i = pl.program_id(1)
    pad = 32

    @pl.when(i == 0)
    def _():
        u_scr[0:pad, :] = jnp.zeros((pad, CONV_CH), F32)
        u_scr[pad + t:, :] = jnp.zeros((SUBLANES, CONV_CH), F32)

    z = z_ref[0]
    u_scr[pad:pad + t, :] = z[:, :CONV_CH] * _sigmoid(z[:, CONV_CH:])
    span = t + pad
    for s in range(SUBLANES):
        sh_scr[s, 0:span, :] = u_scr[s:s + span, :]
    off = pad - (CONV_K - 1)

    def rows_block(bi, carry):
        r0 = pl.multiple_of(bi * CONV_ROWS, CONV_ROWS)
        acc = jnp.zeros((CONV_ROWS, CONV_CH), F32) + dwb_ref[...]
        for j in range(CONV_K):
            a, s = divmod(off + j, SUBLANES)
            acc = acc + dw_ref[j:j + 1, :] * sh_scr[s, pl.ds(r0 + a * SUBLANES, CONV_ROWS), :]
        acc_scr[pl.ds(r0, CONV_ROWS), :] = acc
        return carry

    lax.fori_loop(0, t // CONV_ROWS, rows_block, 0)
    tail = u_scr[t:t + pad, :]
    nb_ref[0] = tail
    u_scr[0:pad, :] = tail
    c = _layer_norm(acc_scr[...], lng_ref[...], lnb_ref[...])
    c = c * _sigmoid(c)
    y_ref[0] = _dot(c, wout_ref[...]).astype(y_ref.dtype)


def _conv_prompt(z, prm, *, t=512):
    B, L, _ = z.shape
    assert Z_C0 % (2 * CONV_CH) == 0
    t = min(t, L)
    assert L % t == 0 and t >= 32 and t % CONV_ROWS == 0
    row = lambda x: x.reshape(1, -1)
    full = lambda a: pl.BlockSpec(a.shape, lambda b, i: (0,) * a.ndim)
    dw = jnp.pad(prm["conv_dw"], ((0, 32 - CONV_K), (0, 0)))
    params = [dw, row(prm["conv_dw_b"]), row(prm["conv_ln_g"]), row(prm["conv_ln_b"]), prm["w_conv_out_bf"]]
    d_out = prm["w_conv_out_bf"].shape[1]
    y, nb = pl.pallas_call(
        functools.partial(_conv_prompt_kernel, t=t),
        out_shape=(jax.ShapeDtypeStruct((B, L, d_out), BF16), jax.ShapeDtypeStruct((B, 32, CONV_CH), F32)),
        grid=(B, L // t),
        in_specs=[pl.BlockSpec((1, t, 2 * CONV_CH), lambda b, i: (b, i, Z_C0 // (2 * CONV_CH)))]
                 + [full(a) for a in params],
        out_specs=(pl.BlockSpec((1, t, d_out), lambda b, i: (b, i, 0)),
                   pl.BlockSpec((1, 32, CONV_CH), lambda b, i: (b, 0, 0))),
        scratch_shapes=[pltpu.VMEM((32 + t + SUBLANES, CONV_CH), F32),
                        pltpu.VMEM((SUBLANES, 32 + t, CONV_CH), F32),
                        pltpu.VMEM((t, CONV_CH), F32)],
        compiler_params=_cparams(("parallel", "arbitrary")),
        name="conv_prompt",
    )(z, *params)
    return y, nb[:, 32 - (CONV_K - 1):]


def _conv_sample_kernel(z_ref, buf_ref, dw_ref, dwb_ref, lng_ref, lnb_ref, c_ref, u_ref):
    z = z_ref[...]
    u = z[:, :CONV_CH] * _sigmoid(z[:, CONV_CH:])
    u_ref[...] = u
    acc = dwb_ref[...] + dw_ref[CONV_K - 1:CONV_K, :] * u
    for j in range(CONV_K - 1):
        acc = acc + dw_ref[j:j + 1, :] * buf_ref[:, j, :]
    c = _layer_norm(acc, lng_ref[...], lnb_ref[...])
    c_ref[...] = c * _sigmoid(c)


def _conv_sample(z_c, buf, prm):
    B = z_c.shape[0]
    row = lambda x: x.reshape(1, -1)
    args = [z_c, buf, prm["conv_dw"], row(prm["conv_dw_b"]), row(prm["conv_ln_g"]), row(prm["conv_ln_b"])]
    return pl.pallas_call(
        _conv_sample_kernel,
        out_shape=(jax.ShapeDtypeStruct((B, CONV_CH), F32), jax.ShapeDtypeStruct((B, CONV_CH), F32)),
        compiler_params=pltpu.CompilerParams(vmem_limit_bytes=VMEM_LIMIT),
        name="conv_sample",
    )(*args)


def _attn_group(q_ref, k_ref, v_ref, o_ref, m_scr, l_scr, acc_scr, *, seq, dil, nk, first, last):
    nb = (seq // dil) // Q_BLOCK
    qi = _iota((Q_BLOCK, 2 * Q_BLOCK), 0)
    kj = _iota((Q_BLOCK, 2 * Q_BLOCK), 1)
    dist = qi + Q_BLOCK - kj
    in_win = (dist >= 0) & (dist <= nk)
    is_cur = kj >= Q_BLOCK
    lane_lo = _iota((Q_BLOCK, LANES), 1) < HEAD_DIM

    def rows(start):
        return pl.ds(start, Q_BLOCK) if dil == 1 else pl.ds(start, Q_BLOCK, stride=dil)

    def body(it, carry):
        blocks = []
        for u in range(ATTN_UNROLL):
            blk = it * ATTN_UNROLL + u
            d = blk // nb
            n = blk - d * nb
            q0 = d + n * (Q_BLOCK * dil)
            p0 = jnp.maximum(q0 - Q_BLOCK * dil, d)
            blocks.append((q0, p0, in_win & (is_cur | (n > 0))))
        qb = [q_ref[0, rows(q0), :] * ATTN_SCALE for q0, _, _ in blocks]
        kb = [jnp.concatenate([k_ref[0, rows(p0), :], k_ref[0, rows(q0), :]], axis=0).astype(BF16)
              for q0, p0, _ in blocks]
        vb = [jnp.concatenate([v_ref[0, rows(p0), :], v_ref[0, rows(q0), :]], axis=0).astype(BF16)
              for q0, p0, _ in blocks]
        units = [(u, lo) for u in range(ATTN_UNROLL) for lo in (True, False)]
        s = [_dot_nt(jnp.where(lane_lo == lo, qb[u], 0.0), kb[u]) for u, lo in units]
        s = [jnp.where(blocks[u][2], x, NEG_BIG) for (u, _), x in zip(units, s)]
        m = [jnp.max(x, axis=1, keepdims=True) for x in s]
        p = [jnp.exp(x - mm) for x, mm in zip(s, m)]
        l = [jnp.sum(x, axis=1, keepdims=True) for x in p]
        o = [jnp.dot(x.astype(BF16), vb[u], preferred_element_type=F32) for (u, _), x in zip(units, p)]
        for u in range(ATTN_UNROLL):
            m_b = jnp.where(lane_lo, m[2 * u], m[2 * u + 1])
            l_b = jnp.where(lane_lo, l[2 * u], l[2 * u + 1])
            o_b = jnp.where(lane_lo, o[2 * u], o[2 * u + 1])
            r = rows(blocks[u][0])
            if not first:
                m_old = m_scr[r, :]
                m_new = jnp.maximum(m_old, m_b)
                a_old = jnp.exp(m_old - m_new)
                a_new = jnp.exp(m_b - m_new)
                l_b = l_scr[r, :] * a_old + l_b * a_new
                o_b = acc_scr[r, :] * a_old + o_b * a_new
                m_b = m_new
            if last:
                o_ref[0, r, :] = o_b / l_b
            else:
                m_scr[r, :] = m_b
                l_scr[r, :] = l_b
                acc_scr[r, :] = o_b
        return carry

    lax.fori_loop(0, seq // (Q_BLOCK * ATTN_UNROLL), body, 0)


def _attn_prompt_kernel(q_ref, k_ref, v_ref, o_ref, m_scr, l_scr, acc_scr, *, seq):
    g = pl.program_id(2)
    for gi, (win, dil) in enumerate(SWA_GROUPS):
        @pl.when(g == gi)
        def _(dil=dil, nk=win // dil, gi=gi):
            _attn_group(q_ref, k_ref, v_ref, o_ref, m_scr, l_scr, acc_scr, seq=seq, dil=dil, nk=nk,
                        first=gi == 0, last=gi == N_GROUPS - 1)


def _attn_prompt(z):
    B, S, _ = z.shape
    assert Z_QKV0 % LANES == 0
    for win, dil in SWA_GROUPS:
        assert S % (dil * Q_BLOCK) == 0 and win // dil <= Q_BLOCK
    assert (S // Q_BLOCK) % ATTN_UNROLL == 0
    pairs = G_WIDTH // LANES
    blk = lambda base: pl.BlockSpec((1, S, LANES), lambda b, p, g: (b, 0, Z_QKV0 // LANES + base + g * pairs + p))
    return pl.pallas_call(
        functools.partial(_attn_prompt_kernel, seq=S),
        out_shape=jax.ShapeDtypeStruct((B, S, G_WIDTH), F32),
        grid=(B, pairs, N_GROUPS),
        in_specs=[blk(0), blk(A_WIDTH // LANES), blk(2 * A_WIDTH // LANES)],
        out_specs=pl.BlockSpec((1, S, LANES), lambda b, p, g: (b, 0, p)),
        scratch_shapes=[pltpu.VMEM((S, LANES), F32)] * 3,
        compiler_params=_cparams(("parallel", "parallel", "arbitrary")),
        name="attn_prompt",
    )(z, z, z)


def _attn_sample_kernel(q_ref, kn_ref, vn_ref, ca_ref, cb_ref, cc_ref, o_ref, *, bs):
    caches = (ca_ref, cb_ref, cc_ref)
    hrow = _iota((SUBLANES, G_WIDTH), 0)
    hmask = hrow == (_iota((SUBLANES, G_WIDTH), 1) >> 6)
    outs = []
    for n in range(bs):
        m_run = l_run = acc = None
        for gi, (win, dil) in enumerate(SWA_GROUPS):
            gs = slice(gi * G_WIDTH, (gi + 1) * G_WIDTH)
            q = q_ref[0, n:n + 1, gs] * ATTN_SCALE
            qr = jnp.where(hmask, q, 0.0).astype(BF16)
            k_t = caches[gi][0, n, 0].reshape(G_WIDTH, win).astype(BF16)
            v_t = caches[gi][0, n, 1].reshape(G_WIDTH, win).astype(BF16)
            kn = kn_ref[0, n:n + 1, gs].astype(BF16).astype(F32)
            vn = vn_ref[0, n:n + 1, gs].astype(BF16).astype(F32)
            valid = (_iota((SUBLANES, win), 1) & (dil - 1)) == 0
            s = jnp.where(valid, jnp.dot(qr, k_t, preferred_element_type=F32), NEG_BIG)
            s_n = jnp.sum(qr.astype(F32) * kn, axis=1, keepdims=True)
            m = jnp.maximum(jnp.max(s, axis=1, keepdims=True), s_n)
            p = jnp.exp(s - m)
            p_n = jnp.exp(s_n - m)
            l = jnp.sum(p, axis=1, keepdims=True) + p_n
            o = _dot_nt(p, v_t) + p_n.astype(BF16).astype(F32) * vn
            if m_run is None:
                m_run, l_run, acc = m, l, o
            else:
                m_new = jnp.maximum(m_run, m)
                a_old = jnp.exp(m_run - m_new)
                a_new = jnp.exp(m - m_new)
                l_run = l_run * a_old + l * a_new
                acc = acc * a_old + o * a_new
                m_run = m_new
        outs.append(jnp.sum(jnp.where(hmask, acc / l_run, 0.0), axis=0, keepdims=True))
    o_ref[0] = jnp.concatenate(outs, axis=0)


def _attn_sample(q, k_new, v_new, caches_t, layer, *, bs=2):
    B = q.shape[0]
    assert B % bs == 0
    for (win, dil), c in zip(SWA_GROUPS, caches_t):
        assert c.shape[-1] == win and win % dil == 0 and dil & (dil - 1) == 0
    vec = pl.BlockSpec((1, bs, A_WIDTH), lambda i: (i, 0, 0))
    cache_spec = lambda c: pl.BlockSpec((1, bs) + c.shape[2:], lambda i: (layer, i, 0, 0, 0, 0))
    rows = lambda x: x.reshape(B // bs, bs, A_WIDTH)
    out = pl.pallas_call(
        functools.partial(_attn_sample_kernel, bs=bs),
        out_shape=jax.ShapeDtypeStruct((B // bs, bs, G_WIDTH), F32),
        grid=(B // bs,),
        in_specs=[vec, vec, vec] + [cache_spec(c) for c in caches_t],
        out_specs=pl.BlockSpec((1, bs, G_WIDTH), lambda i: (i, 0, 0)),
        compiler_params=_cparams(("parallel",)),
        name="attn_sample",
    )(rows(q), rows(k_new), rows(v_new), *caches_t)
    return out.reshape(B, G_WIDTH)


def _cache_shift_kernel(new_ref, ca_ref, cb_ref, cc_ref, oa_ref, ob_ref, oc_ref, *, bs):
    rows = 2 * G_WIDTH
    for gi, (c_ref, o_ref) in enumerate(((ca_ref, oa_ref), (cb_ref, ob_ref), (cc_ref, oc_ref))):
        win = c_ref.shape[-1]
        last = _iota((rows, win), 1) == win - 1
        for n in range(bs):
            x = c_ref[0, n].reshape(rows, win)
            col = new_ref[0, gi * rows:(gi + 1) * rows, n:n + 1]
            o_ref[0, n] = jnp.where(last, col, pltpu.roll(x, win - 1, 1)).reshape(o_ref.shape[2:])


def _cache_shift(caches_t, new_cols, *, bs=2):
    depth, B = caches_t[0].shape[:2]
    assert B % bs == 0
    nb = B // bs
    spec = lambda c: pl.BlockSpec((1, bs) + c.shape[2:], lambda l, i: (l, i, 0, 0, 0, 0))
    n_rows = new_cols.shape[1]
    new_b = new_cols.reshape(depth, n_rows, nb, bs).transpose(0, 2, 1, 3).reshape(depth * nb, n_rows, bs)
    return pl.pallas_call(
        functools.partial(_cache_shift_kernel, bs=bs),
        out_shape=tuple(jax.ShapeDtypeStruct(c.shape, c.dtype) for c in caches_t),
        grid=(depth, nb),
        in_specs=[pl.BlockSpec((1, n_rows, bs), lambda l, i: (l * nb + i, 0, 0))] + [spec(c) for c in caches_t],
        out_specs=tuple(spec(c) for c in caches_t),
        compiler_params=_cparams(("parallel", "parallel")),
        name="cache_shift",
    )(new_b, *caches_t)


def _merge_kernel(gt_ref, yr_ref, yc_ref, ao_ref, x_ref, wa_ref, wo_ref, g_ref, b_ref, h_ref, *, alpha):
    d = x_ref.shape[1]
    gate = lambda i: gt_ref[:, i * d:(i + 1) * d].astype(F32)
    y_a = _dot(ao_ref[...], wa_ref[...])
    merged = gate(0) * yr_ref[...].astype(F32) + gate(1) * yc_ref[...].astype(F32) + gate(2) * y_a
    h_ref[...] = _layer_norm(alpha * x_ref[...] + _dot(merged, wo_ref[...]), g_ref[...], b_ref[...])


def _merge(gates, y_r, y_c, a_o, x, prm, *, alpha, t=512):
    M, D = x.shape
    t = min(t, M)
    assert M % t == 0
    row = lambda a: a.reshape(1, -1)
    tile = lambda a: pl.BlockSpec((t, a.shape[1]), lambda i: (i, 0))
    full = lambda a: pl.BlockSpec(a.shape, lambda i: (0, 0))
    acts = [gates, y_r, y_c, a_o, x]
    params = [prm["w_attn_out_bf"], prm["w_o_bf"], row(prm["ln1_g"]), row(prm["ln1_b"])]
    return pl.pallas_call(
        functools.partial(_merge_kernel, alpha=alpha),
        out_shape=jax.ShapeDtypeStruct((M, D), F32),
        grid=(M // t,),
        in_specs=[tile(a) for a in acts] + [full(a) for a in params],
        out_specs=pl.BlockSpec((t, D), lambda i: (i, 0)),
        compiler_params=_cparams(("parallel",)),
        name="merge",
    )(*acts, *params)


def _ffn_kernel(*refs, alpha, t, tiles_per_seq, sample):
    if sample:
        (h_ref, pg_ref, pu_ref, wg_ref, wu_ref, dg_ref, du_ref, bg_ref, bu_ref, wo_ref, g_ref, b_ref,
         o_ref, ng_ref, nu_ref, acc_scr) = refs
    else:
        (h_ref, wg_ref, wu_ref, dg_ref, du_ref, bg_ref, bu_ref, wo_ref, g_ref, b_ref,
         o_ref, ng_ref, nu_ref, acc_scr, cg_scr, cu_scr) = refs
    i = pl.program_id(0)
    j = pl.program_id(1)
    nj = pl.num_programs(1)

    @pl.when(j == 0)
    def _():
        acc_scr[...] = jnp.zeros_like(acc_scr)

    def conv(u, dw_ref, bias_ref, p1, p2):
        return dw_ref[0:1, :] * p2 + dw_ref[1:2, :] * p1 + dw_ref[2:3, :] * u + bias_ref[...]

    if sample:
        hb = h_ref[...].astype(BF16)
        y = []
        for w_ref, dw_ref, bias_ref, new_ref, prev_ref in ((wg_ref, dg_ref, bg_ref, ng_ref, pg_ref),
                                                           (wu_ref, du_ref, bu_ref, nu_ref, pu_ref)):
            u = jnp.dot(hb, w_ref[...], preferred_element_type=F32)
            new_ref[...] = u
            y.append(conv(u, dw_ref, bias_ref, prev_ref[1], prev_ref[0]))
        acc_scr[...] += _dot((y[0] * _sigmoid(y[0])) * y[1], wo_ref[...])
    else:
        tp = t // FFN_ROW_PARTS
        fresh = (i % tiles_per_seq) == 0
        row = _iota((SUBLANES, wg_ref.shape[1]), 0)
        hb = [h_ref[s * tp:(s + 1) * tp, :].astype(BF16) for s in range(FFN_ROW_PARTS)]
        branches = ((wg_ref, dg_ref, bg_ref, ng_ref, cg_scr), (wu_ref, du_ref, bu_ref, nu_ref, cu_scr))
        u = [[jnp.dot(hb[s], w_ref[...], preferred_element_type=F32) for w_ref, *_ in branches]
             for s in range(FFN_ROW_PARTS)]
        for s in range(FFN_ROW_PARTS):
            y = []
            for bi, (_, dw_ref, bias_ref, new_ref, c_scr) in enumerate(branches):
                us = u[s][bi]
                if s == 0:
                    c1 = jnp.where(fresh, 0.0, c_scr[j, 7:8, :])
                    c2 = jnp.where(fresh, 0.0, c_scr[j, 6:7, :])
                else:
                    c1 = u[s - 1][bi][tp - 1:tp, :]
                    c2 = u[s - 1][bi][tp - 2:tp - 1, :]
                r1 = pltpu.roll(us, 1, 0)
                r2 = pltpu.roll(us, 2, 0)
                p1 = jnp.concatenate([jnp.where(row == 0, c1, r1[:SUBLANES]), r1[SUBLANES:]], axis=0)
                p2 = jnp.concatenate([jnp.where(row == 0, c2, jnp.where(row == 1, c1, r2[:SUBLANES])),
                                      r2[SUBLANES:]], axis=0)
                y.append(conv(us, dw_ref, bias_ref, p1, p2))
                if s == FFN_ROW_PARTS - 1:
                    new_ref[0] = us[tp - 2:tp, :]
                    c_scr[j, :, :] = us[tp - SUBLANES:tp, :]
            acc_scr[s * tp:(s + 1) * tp, :] += _dot((y[0] * _sigmoid(y[0])) * y[1], wo_ref[...])

    @pl.when(j == nj - 1)
    def _():
        o_ref[...] = _layer_norm(alpha * h_ref[...] + acc_scr[...], g_ref[...], b_ref[...])


def _ffn(h, prm, *, alpha, seq_len, prev=None, t=512, tn=2816):
    M, D = h.shape
    d_ff = prm["w_ffn_out_bf"].shape[0]
    sample = prev is not None
    t = M if sample else min(t, seq_len)
    tn = min(tn, d_ff)
    assert M % t == 0 and d_ff % tn == 0 and (sample or (seq_len % t == 0 and t >= 8))
    J = d_ff // tn
    row = lambda a: a.reshape(1, -1)
    w_in, dw, dwb = prm["w_ffn_in_bf"], prm["ffn_dw"], row(prm["ffn_dw_b"])
    col_g = lambda shape: pl.BlockSpec(shape, lambda i, j: (0, j))
    col_u = lambda shape: pl.BlockSpec(shape, lambda i, j: (0, J + j))
    const = lambda a: pl.BlockSpec(a.shape, lambda i, j: (0, 0))
    ins, specs = [h], [pl.BlockSpec((t, D), lambda i, j: (i, 0))]
    if sample:
        pstack = jnp.stack(prev)
        ins += [pstack, pstack]
        specs += [pl.BlockSpec((2, M, tn), lambda i, j: (0, 0, j)), pl.BlockSpec((2, M, tn), lambda i, j: (0, 0, J + j))]
    ins += [w_in, w_in, dw, dw, dwb, dwb, prm["w_ffn_out_bf"], row(prm["ln2_g"]), row(prm["ln2_b"])]
    specs += [col_g((D, tn)), col_u((D, tn)), col_g((FFN_K, tn)), col_u((FFN_K, tn)), col_g((1, tn)),
              col_u((1, tn)), pl.BlockSpec((tn, D), lambda i, j: (j, 0)), const(row(prm["ln2_g"])),
              const(row(prm["ln2_b"]))]
    if sample:
        new_shape = jax.ShapeDtypeStruct((M, d_ff), F32)
        new_spec = pl.BlockSpec((M, tn), lambda i, j: (0, j))
        scratch = [pltpu.VMEM((t, D), F32)]
        tiles_per_seq = 1
    else:
        tiles_per_seq = seq_len // t
        new_shape = jax.ShapeDtypeStruct((M // t, 2, d_ff), F32)
        new_spec = pl.BlockSpec((1, 2, tn), lambda i, j: (i, 0, j))
        scratch = [pltpu.VMEM((t, D), F32), pltpu.VMEM((J, 8, tn), F32), pltpu.VMEM((J, 8, tn), F32)]
    out, new_g, new_u = pl.pallas_call(
        functools.partial(_ffn_kernel, alpha=alpha, t=t, tiles_per_seq=tiles_per_seq, sample=sample),
        out_shape=(jax.ShapeDtypeStruct((M, D), F32), new_shape, new_shape),
        grid=(M // t, J),
        in_specs=specs,
        out_specs=(pl.BlockSpec((t, D), lambda i, j: (i, 0)), new_spec, new_spec),
        scratch_shapes=scratch,
        compiler_params=_cparams(("arbitrary", "arbitrary")),
        name="ffn_sample" if sample else "ffn_prompt",
    )(*ins)
    if not sample:
        new_g = new_g[tiles_per_seq - 1::tiles_per_seq]
        new_u = new_u[tiles_per_seq - 1::tiles_per_seq]
    return out, new_g, new_u


def _layer_weights(l, w_in, w_rwkv_out, w_conv_out, w_attn_out, w_o, w_ffn_in, w_ffn_out, small):
    prm = {k: v[l] for k, v in small.items()}
    wi = w_in[l]
    o_c = R_COLS
    o_q = o_c + 2 * CONV_CH
    o_g = o_q + 3 * A_WIDTH
    prm["w_in_bf"] = jnp.concatenate([wi[:, o_g:], wi[:, :o_c], wi[:, o_q:o_g], wi[:, o_c:o_q]], axis=1).astype(BF16)
    prm["w_rwkv_out_bf"] = w_rwkv_out[l].astype(BF16)
    prm["w_conv_out_bf"] = w_conv_out[l].astype(BF16)
    prm["w_attn_out_bf"] = w_attn_out[l].astype(BF16)
    prm["w_o_bf"] = w_o[l].astype(BF16)
    prm["w_ffn_in_bf"] = w_ffn_in[l].astype(BF16)
    prm["w_ffn_out_bf"] = w_ffn_out[l].astype(BF16)
    return prm


def _project_kernel(x_ref, w_ref, gb_ref, g_ref, z_ref, *, n_gate):
    j = pl.program_id(1)
    xw = lambda: jnp.dot(x_ref[...].astype(BF16), w_ref[...], preferred_element_type=F32)

    @pl.when(j < n_gate)
    def _():
        g_ref[...] = _sigmoid(xw() + gb_ref[...]).astype(g_ref.dtype)

    @pl.when(j >= n_gate)
    def _():
        z_ref[...] = xw()


def _project(x2, prm, tm, tn=512):
    M, K = x2.shape
    w = prm["w_in_bf"]
    gate_w = prm["gate_b"].size
    tm = min(tm, M)
    assert M % tm == 0 and gate_w % tn == 0 and Z_WIDTH % tn == 0 and w.shape[1] == gate_w + Z_WIDTH
    n_gate = gate_w // tn
    return pl.pallas_call(
        functools.partial(_project_kernel, n_gate=n_gate),
        out_shape=(jax.ShapeDtypeStruct((M, gate_w), BF16), jax.ShapeDtypeStruct((M, Z_WIDTH), F32)),
        grid=(M // tm, w.shape[1] // tn),
        in_specs=[pl.BlockSpec((tm, K), lambda i, j: (i, 0)),
                  pl.BlockSpec((K, tn), lambda i, j: (0, j)),
                  pl.BlockSpec((1, tn), lambda i, j: (0, jnp.minimum(j, n_gate - 1)))],
        out_specs=(pl.BlockSpec((tm, tn), lambda i, j: (i, jnp.minimum(j, n_gate - 1))),
                   pl.BlockSpec((tm, tn), lambda i, j: (i, jnp.maximum(j - n_gate, 0)))),
        compiler_params=_cparams(("parallel", "arbitrary")),
        name="project",
    )(x2, w, prm["gate_b"].reshape(1, -1))


def _kv_rows(z_qkv, gi):
    k = z_qkv[..., A_WIDTH + gi * G_WIDTH:A_WIDTH + (gi + 1) * G_WIDTH]
    v = z_qkv[..., 2 * A_WIDTH + gi * G_WIDTH:2 * A_WIDTH + (gi + 1) * G_WIDTH]
    kv = jnp.stack([k, v], axis=-2)
    return kv.reshape(kv.shape[:-1] + (G_HEADS, HEAD_DIM))


def _prompt_layer(x, prm, alpha):
    B, L, D = x.shape
    x2 = x.reshape(B * L, D)
    gates, z = _project(x2, prm, tm=2048)
    z = z.reshape(B, L, Z_WIDTH)
    y_r, new_wkv = _rwkv_prompt(z, prm)
    y_c, new_conv = _conv_prompt(z, prm)
    a_o = _attn_prompt(z)
    h = _merge(gates, y_r.reshape(B * L, D), y_c.reshape(B * L, D), a_o.reshape(B * L, -1), x2, prm, alpha=alpha)
    out, nf_g, nf_u = _ffn(h, prm, alpha=alpha, seq_len=L)
    new_kv = [_kv_rows(z[:, L - min(win, L):, Z_QKV0:Z_C0], gi) for gi, (win, _) in enumerate(SWA_GROUPS)]
    return (out.reshape(B, L, D), z[:, L - 1:, :R_COLS], new_wkv, new_conv, new_kv,
            jnp.concatenate([nf_g, nf_u], axis=-1))


def _sample_layer(x, prm, alpha, shift, wkv, conv_buf, ffn_buf, caches_t, layer):
    B, L, D = x.shape
    assert L == 1
    x2 = x.reshape(B, D)
    z_g, z = _project(x2, prm, tm=B)
    z_r, z_qkv, z_c = z[:, :Z_QKV0], z[:, Z_QKV0:Z_C0], z[:, Z_C0:]
    o_r, new_wkv = _rwkv_sample(z_r, shift.reshape(B, R_COLS), wkv, prm)
    y_r = _mm(o_r, prm["w_rwkv_out_bf"], tm=B, tn=1024, out_dtype=BF16)
    c, u = _conv_sample(z_c, conv_buf, prm)
    y_c = _mm(c, prm["w_conv_out_bf"], tm=B, tn=1024, out_dtype=BF16)
    a_o = _attn_sample(z_qkv[:, :A_WIDTH], z_qkv[:, A_WIDTH:2 * A_WIDTH], z_qkv[:, 2 * A_WIDTH:], caches_t, layer)
    h = _merge(z_g, y_r, y_c, a_o, x2, prm, alpha=alpha)
    out, u_g, u_u = _ffn(h, prm, alpha=alpha, seq_len=1, prev=(ffn_buf[:, 0], ffn_buf[:, 1]))
    kv_new = []
    for gi in range(N_GROUPS):
        kv_new += [z_qkv[:, A_WIDTH + gi * G_WIDTH:A_WIDTH + (gi + 1) * G_WIDTH],
                   z_qkv[:, 2 * A_WIDTH + gi * G_WIDTH:2 * A_WIDTH + (gi + 1) * G_WIDTH]]
    return (out.reshape(B, 1, D), z_r.reshape(B, 1, R_COLS), new_wkv, u, jnp.concatenate(kv_new, axis=-1),
            jnp.concatenate([u_g, u_u], axis=-1))


def _append_row(buf, rows):
    return jnp.concatenate([buf[:, :, 1:], rows[:, :, None]], axis=2)


def kernel(x_prompt, x_sample, state_shift, state_wkv, state_conv, cache_swa_a, cache_swa_b, cache_swa_c, state_ffn, w_in, rwkv_mu, rwkv_w0, rwkv_w_up, rwkv_a0, rwkv_a_up, rwkv_g_up, rwkv_k_k, rwkv_k_a, rwkv_r_k, rwkv_ln_g, rwkv_ln_b, w_rwkv_out, conv_dw, conv_dw_b, conv_ln_g, conv_ln_b, w_conv_out, w_attn_out, gate_b, w_o, ln1_g, ln1_b, w_ffn_in, ffn_dw, ffn_dw_b, w_ffn_out, ln2_g, ln2_b):
    depth = w_in.shape[0]
    alpha = (2 * depth) ** 0.25
    small = dict(rwkv_mu=rwkv_mu, rwkv_w0=rwkv_w0, rwkv_w_up=rwkv_w_up, rwkv_a0=rwkv_a0, rwkv_a_up=rwkv_a_up,
                 rwkv_g_up=rwkv_g_up, rwkv_k_k=rwkv_k_k, rwkv_k_a=rwkv_k_a,
                 rwkv_r_k=rwkv_r_k.reshape(depth, -1), rwkv_ln_g=rwkv_ln_g, rwkv_ln_b=rwkv_ln_b,
                 conv_dw=conv_dw, conv_dw_b=conv_dw_b, conv_ln_g=conv_ln_g, conv_ln_b=conv_ln_b,
                 gate_b=gate_b.reshape(depth, -1), ln1_g=ln1_g, ln1_b=ln1_b, ffn_dw=ffn_dw, ffn_dw_b=ffn_dw_b,
                 ln2_g=ln2_g, ln2_b=ln2_b)
    caches = (cache_swa_a, cache_swa_b, cache_swa_c)
    caches_t = [jnp.transpose(c, (0, 1, 3, 4, 5, 2)) for c in caches]
    hp, hs = x_prompt, x_sample
    outs_p, outs_s = [], []
    for l in range(depth):
        prm = _layer_weights(l, w_in, w_rwkv_out, w_conv_out, w_attn_out, w_o, w_ffn_in, w_ffn_out, small)
        res_p = _prompt_layer(hp, prm, alpha)
        hp = res_p[0]
        outs_p.append(res_p[1:])
        res_s = _sample_layer(hs, prm, alpha, state_shift[l], state_wkv[l], state_conv[l], state_ffn[l],
                              caches_t, l)
        hs = res_s[0]
        outs_s.append(res_s[1:])
    stk = lambda outs, f: jnp.stack([f(o) for o in outs])
    res = [hp, hs]
    res += [stk(outs_p, lambda o: o[0]), stk(outs_s, lambda o: o[0])]
    res += [stk(outs_p, lambda o: o[1]), stk(outs_s, lambda o: o[1])]
    res += [stk(outs_p, lambda o: o[2]), _append_row(state_conv, stk(outs_s, lambda o: o[2]))]
    new_cols = jnp.transpose(stk(outs_s, lambda o: o[3]), (0, 2, 1))
    shifted = _cache_shift(caches_t, new_cols)
    for gi in range(N_GROUPS):
        res += [stk(outs_p, lambda o: o[3][gi]), jnp.transpose(shifted[gi], (0, 1, 5, 2, 3, 4))]
    res += [stk(outs_p, lambda o: o[4]), _append_row(state_ffn, stk(outs_s, lambda o: o[4]))]
    return tuple(res)
```

```python
import functools

import jax
import jax.numpy as jnp
from jax import lax
from jax.experimental import pallas as pl
from jax.experimental.pallas import tpu as pltpu

F32 = jnp.float32
BF16 = jnp.bfloat16

HEAD_DIM = 64
R_HEADS = 8
R_WIDTH = R_HEADS * HEAD_DIM
DECAY_LORA = 64
ICLR_LORA = 64
GATE_LORA = 128
R_COLS = 3 * R_WIDTH + DECAY_LORA + ICLR_LORA + GATE_LORA
LNX_EPS = 64e-5
CONV_CH = 512
CONV_K = 31
SWA_GROUPS = ((128, 1), (512, 4), (2048, 16))
N_GROUPS = len(SWA_GROUPS)
G_HEADS = 4
G_WIDTH = G_HEADS * HEAD_DIM
A_WIDTH = N_GROUPS * G_WIDTH
Q_BLOCK = 128
ATTN_SCALE = HEAD_DIM ** -0.5
N_BRANCH = 3
FFN_K = 3
LN_EPS = 1e-5
NEG_BIG = -1e30

Z_QKV0 = R_COLS
Z_C0 = Z_QKV0 + 3 * A_WIDTH
Z_WIDTH = Z_C0 + 2 * CONV_CH

LANES = 128
CHUNK = 64
SUBLANES = 8
CONV_ROWS = 32
SEG_BLOCK = 256
FFN_ROW_PARTS = 2
ATTN_UNROLL = 8
VMEM_LIMIT = 56 * 1024 * 1024


def _cparams(sem):
    return pltpu.CompilerParams(dimension_semantics=sem, vmem_limit_bytes=VMEM_LIMIT)


def _dot(a, b):
    return jnp.dot(a.astype(BF16), b.astype(BF16), preferred_element_type=F32)


def _dot_nt(a, b):
    return lax.dot_general(a.astype(BF16), b.astype(BF16), (((1,), (1,)), ((), ())),
                           preferred_element_type=F32)


def _dot_tn(a, b):
    return lax.dot_general(a.astype(BF16), b.astype(BF16), (((0,), (0,)), ((), ())),
                           preferred_element_type=F32)


def _split3(x):
    h1 = x.astype(BF16)
    r1 = x - h1.astype(F32)
    h2 = r1.astype(BF16)
    h3 = (r1 - h2.astype(F32)).astype(BF16)
    return h1, h2, h3


def _sigmoid(x):
    return 1.0 / (1.0 + jnp.exp(-x))


def _layer_norm(x, g, b):
    mu = jnp.mean(x, axis=-1, keepdims=True)
    xc = x - mu
    var = jnp.mean(xc * xc, axis=-1, keepdims=True)
    return xc * lax.rsqrt(var + LN_EPS) * g + b


def _iota(shape, dim):
    return lax.broadcasted_iota(jnp.int32, shape, dim)


def _head_ones():
    r = _iota((SEG_BLOCK, SEG_BLOCK), 0)
    c = _iota((SEG_BLOCK, SEG_BLOCK), 1)
    return jnp.where((r >> 6) == (c >> 6), 1.0, 0.0).astype(BF16)


def _segsum(x, ones):
    outs = [_dot(x[:, p:p + SEG_BLOCK], ones) for p in range(0, x.shape[1], SEG_BLOCK)]
    return outs[0] if len(outs) == 1 else jnp.concatenate(outs, axis=1)


def _mm_kernel(x_ref, w_ref, o_ref):
    o_ref[...] = jnp.dot(x_ref[...].astype(BF16), w_ref[...],
                         preferred_element_type=F32).astype(o_ref.dtype)


def _mm(x, w, *, tm, tn, out_dtype=F32):
    M, K = x.shape
    N = w.shape[1]
    tm = min(tm, M)
    tn = min(tn, N)
    assert M % tm == 0 and N % tn == 0
    return pl.pallas_call(
        _mm_kernel,
        out_shape=jax.ShapeDtypeStruct((M, N), out_dtype),
        grid=(M // tm, N // tn),
        in_specs=[pl.BlockSpec((tm, K), lambda i, j: (i, 0)),
                  pl.BlockSpec((K, tn), lambda i, j: (0, j))],
        out_specs=pl.BlockSpec((tm, tn), lambda i, j: (i, j)),
        compiler_params=_cparams(("parallel", "parallel")),
        name="mm",
    )(x, w)


def _rwkv_prep(zr, prevs, mu, w0, w_up, a0, a_up, g_up, k_k, k_a, ones):
    zs = zr + (prevs - zr) * mu
    r = zs[:, 0:R_WIDTH]
    k = zs[:, R_WIDTH:2 * R_WIDTH]
    v = zs[:, 2 * R_WIDTH:3 * R_WIDTH]
    o1 = 3 * R_WIDTH
    wd = zs[:, o1:o1 + DECAY_LORA]
    ad = zs[:, o1 + DECAY_LORA:o1 + DECAY_LORA + ICLR_LORA]
    gd = zs[:, o1 + DECAY_LORA + ICLR_LORA:R_COLS]
    nx = -(w0 + _dot(jnp.tanh(wd), w_up))
    softplus = jnp.maximum(nx, 0.0) + jnp.log(1.0 + jnp.exp(-jnp.abs(nx)))
    logdec = -jnp.exp(-softplus - 0.5)
    a = _sigmoid(a0 + _dot(ad, a_up))
    g = _dot(_sigmoid(gd), g_up)
    kk = k * k_k
    kk = kk / jnp.maximum(jnp.sqrt(_segsum(kk * kk, ones)), 1e-12)
    k2 = k * (1.0 + (a - 1.0) * k_a)
    return r, k2, v, kk, a, g, logdec


def _rwkv_post(o, r, k2, v, g, r_k, ln_g, ln_b, ones):
    inv = 1.0 / HEAD_DIM
    m = _segsum(o, ones) * inv
    oc = o - m
    var = _segsum(oc * oc, ones) * inv
    on = oc * lax.rsqrt(var + LNX_EPS) * ln_g + ln_b
    bonus = _segsum(r * k2 * r_k, ones) * v
    return (on + bonus) * g


def _chunk_cumsum(logdec, tri_bd):
    h1, h2, h3 = _split3(logdec)
    d = lambda x: jnp.dot(tri_bd, x, preferred_element_type=F32)
    return d(h1) + d(h2) + d(h3)


def _rwkv_chunk_tables(logdec, cum_all, r, k2, v, kk, a, c):
    sl = slice(c * CHUNK, (c + 1) * CHUNK)
    cum = cum_all[sl]
    ld = logdec[sl]
    cum_end = cum[CHUNK - 1:CHUNK, :]
    p_in = jnp.exp(cum)
    p_ex = jnp.exp(cum - ld)
    p_inv = jnp.exp(-cum)
    p_tail = jnp.exp(cum_end - cum)
    kka = kk[sl] * a[sl]
    return dict(rt=r[sl] * p_in, at=-kk[sl] * p_ex, bt=kka * p_inv, kt=k2[sl] * p_inv,
                bh=kka * p_tail, kh=k2[sl] * p_tail, v=v[sl], pc=jnp.exp(cum_end))


def _rwkv_units(tables, n_pairs, masks):
    strict, incl, eye, lane_lo, row_lo, lane_lo_n = masks
    n = 2 * CHUNK
    units = [(tb, slice(p * LANES, (p + 1) * LANES)) for tb in tables for p in range(n_pairs)]
    swap = lambda x: pltpu.roll(x, HEAD_DIM, 1)
    masked = lambda x: jnp.concatenate([jnp.where(lane_lo, x, 0.0), jnp.where(lane_lo, 0.0, x)], axis=0)
    local_lo = lambda x: jnp.concatenate([x, swap(x)], axis=0)
    local_hi = lambda x: jnp.concatenate([swap(x), x], axis=0)
    twice = lambda x: jnp.concatenate([x, x], axis=0)

    def spread(x):
        return jnp.where(row_lo, jnp.where(lane_lo_n, x, 0.0), jnp.where(lane_lo_n, 0.0, swap(x)))

    aa, a_lo, v_hi, r_m, bk_m = [], [], [], [], []
    for tb, ls in units:
        r_m.append(masked(tb["rt"][:, ls]))
        aa.append(_dot_nt(jnp.concatenate([masked(tb["at"][:, ls]), r_m[-1]], axis=0),
                          jnp.concatenate([twice(tb["bt"][:, ls]), twice(tb["kt"][:, ls])], axis=0)))
        a_lo.append(local_lo(tb["at"][:, ls]))
        v_hi.append(local_hi(tb["v"][:, ls]))
        bk_m.append(jnp.concatenate([masked(tb["bh"][:, ls]), masked(tb["kh"][:, ls])], axis=0))
    lj = [jnp.where(strict, a[:n, :n], 0.0) for a in aa]
    x = [jnp.where(lane_lo_n, al, _dot(jnp.where(strict, a[:n, n:], 0.0), vh))
         for al, vh, a in zip(a_lo, v_hi, aa)]
    steps = CHUNK.bit_length() - 1
    for j in range(steps):
        if j < steps - 1:
            prod = [_dot(l, jnp.concatenate([l, xx], axis=1)) for l, xx in zip(lj, x)]
            lj = [pr[:, :n] for pr in prod]
            x = [xx + pr[:, n:] for xx, pr in zip(x, prod)]
        else:
            x = [xx + _dot(l, xx) for l, xx in zip(lj, x)]
    rhs = [jnp.concatenate([xx, jnp.where(lane_lo_n, 0.0, vh)], axis=0) for xx, vh in zip(x, v_hi)]
    qo = [_dot(jnp.concatenate([jnp.where(incl, a[n:, :n], 0.0), jnp.where(incl, a[n:, n:], 0.0)], axis=1), rh)
          for a, rh in zip(aa, rhs)]
    gh = [_dot_tn(bk, rh) for bk, rh in zip(bk_m, rhs)]
    return [(rm + spread(q), q, spread(g) + jnp.where(eye, tb["pc"][:, ls], 0.0), g)
            for (tb, ls), rm, q, g in zip(units, r_m, qo, gh)]


def _rwkv_masks():
    n = 2 * CHUNK
    rr = _iota((n, n), 0)
    cc = _iota((n, n), 1)
    same = (rr >> 6) == (cc >> 6)
    strict = same & ((cc & 63) < (rr & 63))
    incl = same & ((cc & 63) <= (rr & 63))
    eye = rr == cc
    lane_lo = _iota((CHUNK, LANES), 1) < HEAD_DIM
    return strict, incl, eye, lane_lo, rr < CHUNK, cc < HEAD_DIM


def _rwkv_prompt_kernel(z_ref, mu_ref, w0_ref, wup_ref, a0_ref, aup_ref, gup_ref, kk_ref, ka_ref,
                        rk_ref, lng_ref, lnb_ref, wout_ref, y_ref, st_ref, prev_scr, st_scr, *, ct):
    i = pl.program_id(1)

    @pl.when(i == 0)
    def _():
        prev_scr[...] = jnp.zeros_like(prev_scr)
        st_scr[...] = jnp.zeros_like(st_scr)

    zr = z_ref[0]
    row = _iota(zr.shape, 0)
    prevs = jnp.where(row == 0, prev_scr[0:1, :], pltpu.roll(zr, 1, 0))
    prev_scr[0:1, :] = zr[ct - 1:ct, :]
    ones = _head_ones()
    r, k2, v, kk, a, g, logdec = _rwkv_prep(
        zr, prevs, mu_ref[...], w0_ref[...], wup_ref[...], a0_ref[...], aup_ref[...], gup_ref[...],
        kk_ref[...], ka_ref[...], ones)

    rr = _iota((ct, ct), 0)
    cc = _iota((ct, ct), 1)
    tri_bd = jnp.where(((rr >> 6) == (cc >> 6)) & (cc <= rr), 1.0, 0.0).astype(BF16)
    masks = _rwkv_masks()
    n_pairs = R_WIDTH // LANES
    cum_all = _chunk_cumsum(logdec, tri_bd)
    n_chunks = ct // CHUNK
    tables = [_rwkv_chunk_tables(logdec, cum_all, r, k2, v, kk, a, c) for c in range(n_chunks)]
    units = _rwkv_units(tables, n_pairs, masks)
    states = [st_scr[p] for p in range(n_pairs)]
    lane_lo, lane_lo_n = masks[3], masks[5]
    o_rows = []
    for c in range(n_chunks):
        cur = units[c * n_pairs:(c + 1) * n_pairs]
        ostk = [_dot(qp, st) + op for (qp, op, _, _), st in zip(cur, states)]
        states = [jnp.where(lane_lo_n, 0.0, _dot(gm, st) + hm) for (_, _, gm, hm), st in zip(cur, states)]
        o_rows.append(jnp.concatenate(
            [jnp.where(lane_lo, pltpu.roll(o[:CHUNK], HEAD_DIM, 1), o[CHUNK:]) for o in ostk], axis=1))
    for p in range(n_pairs):
        st_scr[p] = states[p]
    o = o_rows[0] if len(o_rows) == 1 else jnp.concatenate(o_rows, axis=0)
    out = _rwkv_post(o, r, k2, v, g, rk_ref[...], lng_ref[...], lnb_ref[...], ones)
    y_ref[0] = _dot(out, wout_ref[...]).astype(y_ref.dtype)
    st_ref[0] = st_scr[...]


def _rwkv_prompt(z_r, prm, *, ct=512):
    B, L, _ = z_r.shape
    assert L % ct == 0 and ct % CHUNK == 0
    n_pairs = R_WIDTH // LANES
    row = lambda x: x.reshape(1, -1)
    full = lambda a: pl.BlockSpec(a.shape, lambda b, i: (0,) * a.ndim)
    params = [row(prm["rwkv_mu"]), row(prm["rwkv_w0"]), prm["rwkv_w_up"], row(prm["rwkv_a0"]),
              prm["rwkv_a_up"], prm["rwkv_g_up"], row(prm["rwkv_k_k"]), row(prm["rwkv_k_a"]),
              row(prm["rwkv_r_k"]), row(prm["rwkv_ln_g"]), row(prm["rwkv_ln_b"]), prm["w_rwkv_out_bf"]]
    y, st = pl.pallas_call(
        functools.partial(_rwkv_prompt_kernel, ct=ct),
        out_shape=(jax.ShapeDtypeStruct((B, L, prm["w_rwkv_out_bf"].shape[1]), BF16),
                   jax.ShapeDtypeStruct((B, n_pairs, LANES, LANES), F32)),
        grid=(B, L // ct),
        in_specs=[pl.BlockSpec((1, ct, R_COLS), lambda b, i: (b, i, 0))] + [full(a) for a in params],
        out_specs=(pl.BlockSpec((1, ct, prm["w_rwkv_out_bf"].shape[1]), lambda b, i: (b, i, 0)),
                   pl.BlockSpec((1, n_pairs, LANES, LANES), lambda b, i: (b, 0, 0, 0))),
        scratch_shapes=[pltpu.VMEM((8, R_COLS), F32), pltpu.VMEM((n_pairs, LANES, LANES), F32)],
        compiler_params=_cparams(("parallel", "arbitrary")),
        name="rwkv_prompt",
    )(z_r, *params)
    st = st[..., HEAD_DIM:].reshape(B, R_HEADS, HEAD_DIM, HEAD_DIM)
    return y, jnp.swapaxes(st, -1, -2)


def _rwkv_sample_kernel(z_ref, sh_ref, s_ref, mu_ref, w0_ref, wup_ref, a0_ref, aup_ref, gup_ref,
                        kk_ref, ka_ref, rk_ref, lng_ref, lnb_ref, o_ref, sn_ref):
    ones = _head_ones()
    zr = z_ref[0]
    r, k2, v, kk, a, g, logdec = _rwkv_prep(
        zr, sh_ref[0], mu_ref[...], w0_ref[...], wup_ref[...], a0_ref[...], aup_ref[...],
        gup_ref[...], kk_ref[...], ka_ref[...], ones)
    w = jnp.exp(logdec)
    b = kk * a
    eye = _iota((HEAD_DIM, HEAD_DIM), 0) == _iota((HEAD_DIM, HEAD_DIM), 1)
    col = lambda x: jnp.sum(jnp.where(eye, x, 0.0), axis=1, keepdims=True)
    o_heads = []
    for h in range(R_HEADS):
        hs = slice(h * HEAD_DIM, (h + 1) * HEAD_DIM)
        s = s_ref[0, h]
        sa = jnp.sum(s * (-kk[:, hs]), axis=1, keepdims=True)
        s_new = s * w[:, hs] + sa * b[:, hs] + col(v[:, hs]) * k2[:, hs]
        sn_ref[0, h] = s_new
        o_col = jnp.sum(s_new * r[:, hs], axis=1, keepdims=True)
        o_heads.append(jnp.sum(jnp.where(eye, o_col, 0.0), axis=0, keepdims=True))
    o = jnp.concatenate(o_heads, axis=1)
    o_ref[0] = _rwkv_post(o, r, k2, v, g, rk_ref[...], lng_ref[...], lnb_ref[...], ones)


def _rwkv_sample(z_r, shift, wkv, prm):
    B = z_r.shape[0]
    row = lambda x: x.reshape(1, -1)
    full = lambda a: pl.BlockSpec(a.shape, lambda b: (0,) * a.ndim)
    params = [row(prm["rwkv_mu"]), row(prm["rwkv_w0"]), prm["rwkv_w_up"], row(prm["rwkv_a0"]),
              prm["rwkv_a_up"], prm["rwkv_g_up"], row(prm["rwkv_k_k"]), row(prm["rwkv_k_a"]),
              row(prm["rwkv_r_k"]), row(prm["rwkv_ln_g"]), row(prm["rwkv_ln_b"])]
    o, s_new = pl.pallas_call(
        _rwkv_sample_kernel,
        out_shape=(jax.ShapeDtypeStruct((B, 1, R_WIDTH), F32),
                   jax.ShapeDtypeStruct(wkv.shape, F32)),
        grid=(B,),
        in_specs=[pl.BlockSpec((1, 1, R_COLS), lambda b: (b, 0, 0)),
                  pl.BlockSpec((1, 1, R_COLS), lambda b: (b, 0, 0)),
                  pl.BlockSpec((1,) + wkv.shape[1:], lambda b: (b, 0, 0, 0))] + [full(a) for a in params],
        out_specs=(pl.BlockSpec((1, 1, R_WIDTH), lambda b: (b, 0, 0)),
                   pl.BlockSpec((1,) + wkv.shape[1:], lambda b: (b, 0, 0, 0))),
        compiler_params=_cparams(("parallel",)),
        name="rwkv_sample",
    )(z_r.reshape(B, 1, R_COLS), shift.reshape(B, 1, R_COLS), wkv, *params)
    return o.reshape(B, R_WIDTH), s_new


def _conv_prompt_kernel(z_ref, dw_ref, dwb_ref, lng_ref, lnb_ref, wout_ref, y_ref, nb_ref, u_scr, sh_scr,
                        acc_scr, *, t):
    i = pl.program_id(1)
    pad = 32

    @pl.when(i == 0)
    def _():
        u_scr[0:pad, :] = jnp.zeros((pad, CONV_CH), F32)
        u_scr[pad + t:, :] = jnp.zeros((SUBLANES, CONV_CH), F32)

    z = z_ref[0]
    u_scr[pad:pad + t, :] = z[:, :CONV_CH] * _sigmoid(z[:, CONV_CH:])
    span = t + pad
    for s in range(SUBLANES):
        sh_scr[s, 0:span, :] = u_scr[s:s + span, :]
    off = pad - (CONV_K - 1)

    def rows_block(bi, carry):
        r0 = pl.multiple_of(bi * CONV_ROWS, CONV_ROWS)
        acc = jnp.zeros((CONV_ROWS, CONV_CH), F32) + dwb_ref[...]
        for j in range(CONV_K):
            a, s = divmod(off + j, SUBLANES)
            acc = acc + dw_ref[j:j + 1, :] * sh_scr[s, pl.ds(r0 + a * SUBLANES, CONV_ROWS), :]
        acc_scr[pl.ds(r0, CONV_ROWS), :] = acc
        return carry

    lax.fori_loop(0, t // CONV_ROWS, rows_block, 0)
    tail = u_scr[t:t + pad, :]
    nb_ref[0] = tail
    u_scr[0:pad, :] = tail
    c = _layer_norm(acc_scr[...], lng_ref[...], lnb_ref[...])
    c = c * _sigmoid(c)
    y_ref[0] = _dot(c, wout_ref[...]).astype(y_ref.dtype)


def _conv_prompt(z, prm, *, t=1024):
    B, L, _ = z.shape
    assert Z_C0 % (2 * CONV_CH) == 0
    t = min(t, L)
    assert L % t == 0 and t >= 32 and t % CONV_ROWS == 0
    row = lambda x: x.reshape(1, -1)
    full = lambda a: pl.BlockSpec(a.shape, lambda b, i: (0,) * a.ndim)
    dw = jnp.pad(prm["conv_dw"], ((0, 32 - CONV_K), (0, 0)))
    params = [dw, row(prm["conv_dw_b"]), row(prm["conv_ln_g"]), row(prm["conv_ln_b"]), prm["w_conv_out_bf"]]
    d_out = prm["w_conv_out_bf"].shape[1]
    y, nb = pl.pallas_call(
        functools.partial(_conv_prompt_kernel, t=t),
        out_shape=(jax.ShapeDtypeStruct((B, L, d_out), BF16), jax.ShapeDtypeStruct((B, 32, CONV_CH), F32)),
        grid=(B, L // t),
        in_specs=[pl.BlockSpec((1, t, 2 * CONV_CH), lambda b, i: (b, i, Z_C0 // (2 * CONV_CH)))]
                 + [full(a) for a in params],
        out_specs=(pl.BlockSpec((1, t, d_out), lambda b, i: (b, i, 0)),
                   pl.BlockSpec((1, 32, CONV_CH), lambda b, i: (b, 0, 0))),
        scratch_shapes=[pltpu.VMEM((32 + t + SUBLANES, CONV_CH), F32),
                        pltpu.VMEM((SUBLANES, 32 + t, CONV_CH), F32),
                        pltpu.VMEM((t, CONV_CH), F32)],
        compiler_params=_cparams(("parallel", "arbitrary")),
        name="conv_prompt",
    )(z, *params)
    return y, nb[:, 32 - (CONV_K - 1):]


def _conv_sample_kernel(z_ref, buf_ref, dw_ref, dwb_ref, lng_ref, lnb_ref, c_ref, u_ref):
    z = z_ref[...]
    u = z[:, :CONV_CH] * _sigmoid(z[:, CONV_CH:])
    u_ref[...] = u
    acc = dwb_ref[...] + dw_ref[CONV_K - 1:CONV_K, :] * u
    for j in range(CONV_K - 1):
        acc = acc + dw_ref[j:j + 1, :] * buf_ref[:, j, :]
    c = _layer_norm(acc, lng_ref[...], lnb_ref[...])
    c_ref[...] = c * _sigmoid(c)


def _conv_sample(z_c, buf, prm):
    B = z_c.shape[0]
    row = lambda x: x.reshape(1, -1)
    args = [z_c, buf, prm["conv_dw"], row(prm["conv_dw_b"]), row(prm["conv_ln_g"]), row(prm["conv_ln_b"])]
    return pl.pallas_call(
        _conv_sample_kernel,
        out_shape=(jax.ShapeDtypeStruct((B, CONV_CH), F32), jax.ShapeDtypeStruct((B, CONV_CH), F32)),
        compiler_params=pltpu.CompilerParams(vmem_limit_bytes=VMEM_LIMIT),
        name="conv_sample",
    )(*args)


def _attn_group(q_ref, k_ref, v_ref, o_ref, m_scr, l_scr, acc_scr, *, seq, dil, nk, first, last):
    nb = (seq // dil) // Q_BLOCK
    qi = _iota((Q_BLOCK, 2 * Q_BLOCK), 0)
    kj = _iota((Q_BLOCK, 2 * Q_BLOCK), 1)
    dist = qi + Q_BLOCK - kj
    in_win = (dist >= 0) & (dist <= nk)
    is_cur = kj >= Q_BLOCK
    lane_lo = _iota((Q_BLOCK, LANES), 1) < HEAD_DIM

    def rows(start):
        return pl.ds(start, Q_BLOCK) if dil == 1 else pl.ds(start, Q_BLOCK, stride=dil)

    def body(it, carry):
        blocks = []
        for u in range(ATTN_UNROLL):
            blk = it * ATTN_UNROLL + u
            d = blk // nb
            n = blk - d * nb
            q0 = d + n * (Q_BLOCK * dil)
            p0 = jnp.maximum(q0 - Q_BLOCK * dil, d)
            blocks.append((q0, p0, in_win & (is_cur | (n > 0))))
        qb = [q_ref[0, rows(q0), :] * ATTN_SCALE for q0, _, _ in blocks]
        kb = [jnp.concatenate([k_ref[0, rows(p0), :], k_ref[0, rows(q0), :]], axis=0).astype(BF16)
              for q0, p0, _ in blocks]
        vb = [jnp.concatenate([v_ref[0, rows(p0), :], v_ref[0, rows(q0), :]], axis=0).astype(BF16)
              for q0, p0, _ in blocks]
        units = [(u, lo) for u in range(ATTN_UNROLL) for lo in (True, False)]
        s = [_dot_nt(jnp.where(lane_lo == lo, qb[u], 0.0), kb[u]) for u, lo in units]
        s = [jnp.where(blocks[u][2], x, NEG_BIG) for (u, _), x in zip(units, s)]
        m = [jnp.max(x, axis=1, keepdims=True) for x in s]
        p = [jnp.exp(x - mm) for x, mm in zip(s, m)]
        l = [jnp.sum(x, axis=1, keepdims=True) for x in p]
        o = [jnp.dot(x.astype(BF16), vb[u], preferred_element_type=F32) for (u, _), x in zip(units, p)]
        for u in range(ATTN_UNROLL):
            m_b = jnp.where(lane_lo, m[2 * u], m[2 * u + 1])
            l_b = jnp.where(lane_lo, l[2 * u], l[2 * u + 1])
            o_b = jnp.where(lane_lo, o[2 * u], o[2 * u + 1])
            r = rows(blocks[u][0])
            if not first:
                m_old = m_scr[r, :]
                m_new = jnp.maximum(m_old, m_b)
                a_old = jnp.exp(m_old - m_new)
                a_new = jnp.exp(m_b - m_new)
                l_b = l_scr[r, :] * a_old + l_b * a_new
                o_b = acc_scr[r, :] * a_old + o_b * a_new
                m_b = m_new
            if last:
                o_ref[0, r, :] = o_b / l_b
            else:
                m_scr[r, :] = m_b
                l_scr[r, :] = l_b
                acc_scr[r, :] = o_b
        return carry

    lax.fori_loop(0, seq // (Q_BLOCK * ATTN_UNROLL), body, 0)


def _attn_prompt_kernel(q_ref, k_ref, v_ref, o_ref, m_scr, l_scr, acc_scr, *, seq):
    g = pl.program_id(2)
    for gi, (win, dil) in enumerate(SWA_GROUPS):
        @pl.when(g == gi)
        def _(dil=dil, nk=win // dil, gi=gi):
            _attn_group(q_ref, k_ref, v_ref, o_ref, m_scr, l_scr, acc_scr, seq=seq, dil=dil, nk=nk,
                        first=gi == 0, last=gi == N_GROUPS - 1)


def _attn_prompt(z):
    B, S, _ = z.shape
    assert Z_QKV0 % LANES == 0
    for win, dil in SWA_GROUPS:
        assert S % (dil * Q_BLOCK) == 0 and win // dil <= Q_BLOCK
    assert (S // Q_BLOCK) % ATTN_UNROLL == 0
    pairs = G_WIDTH // LANES
    blk = lambda base: pl.BlockSpec((1, S, LANES), lambda b, p, g: (b, 0, Z_QKV0 // LANES + base + g * pairs + p))
    return pl.pallas_call(
        functools.partial(_attn_prompt_kernel, seq=S),
        out_shape=jax.ShapeDtypeStruct((B, S, G_WIDTH), F32),
        grid=(B, pairs, N_GROUPS),
        in_specs=[blk(0), blk(A_WIDTH // LANES), blk(2 * A_WIDTH // LANES)],
        out_specs=pl.BlockSpec((1, S, LANES), lambda b, p, g: (b, 0, p)),
        scratch_shapes=[pltpu.VMEM((S, LANES), F32)] * 3,
        compiler_params=_cparams(("parallel", "parallel", "arbitrary")),
        name="attn_prompt",
    )(z, z, z)


def _attn_sample_kernel(q_ref, kn_ref, vn_ref, ca_ref, cb_ref, cc_ref, o_ref, *, bs):
    caches = (ca_ref, cb_ref, cc_ref)
    hrow = _iota((SUBLANES, G_WIDTH), 0)
    hmask = hrow == (_iota((SUBLANES, G_WIDTH), 1) >> 6)
    outs = []
    for n in range(bs):
        m_run = l_run = acc = None
        for gi, (win, dil) in enumerate(SWA_GROUPS):
            gs = slice(gi * G_WIDTH, (gi + 1) * G_WIDTH)
            q = q_ref[0, n:n + 1, gs] * ATTN_SCALE
            qr = jnp.where(hmask, q, 0.0).astype(BF16)
            k_t = caches[gi][0, n, 0].reshape(G_WIDTH, win).astype(BF16)
            v_t = caches[gi][0, n, 1].reshape(G_WIDTH, win).astype(BF16)
            kn = kn_ref[0, n:n + 1, gs].astype(BF16).astype(F32)
            vn = vn_ref[0, n:n + 1, gs].astype(BF16).astype(F32)
            valid = (_iota((SUBLANES, win), 1) & (dil - 1)) == 0
            s = jnp.where(valid, jnp.dot(qr, k_t, preferred_element_type=F32), NEG_BIG)
            s_n = jnp.sum(qr.astype(F32) * kn, axis=1, keepdims=True)
            m = jnp.maximum(jnp.max(s, axis=1, keepdims=True), s_n)
            p = jnp.exp(s - m)
            p_n = jnp.exp(s_n - m)
            l = jnp.sum(p, axis=1, keepdims=True) + p_n
            o = _dot_nt(p, v_t) + p_n.astype(BF16).astype(F32) * vn
            if m_run is None:
                m_run, l_run, acc = m, l, o
            else:
                m_new = jnp.maximum(m_run, m)
                a_old = jnp.exp(m_run - m_new)
                a_new = jnp.exp(m - m_new)
                l_run = l_run * a_old + l * a_new
                acc = acc * a_old + o * a_new
                m_run = m_new
        outs.append(jnp.sum(jnp.where(hmask, acc / l_run, 0.0), axis=0, keepdims=True))
    o_ref[0] = jnp.concatenate(outs, axis=0)


def _attn_sample(q, k_new, v_new, caches_t, layer, *, bs=2):
    B = q.shape[0]
    assert B % bs == 0
    for (win, dil), c in zip(SWA_GROUPS, caches_t):
        assert c.shape[-1] == win and win % dil == 0 and dil & (dil - 1) == 0
    vec = pl.BlockSpec((1, bs, A_WIDTH), lambda i: (i, 0, 0))
    cache_spec = lambda c: pl.BlockSpec((1, bs) + c.shape[2:], lambda i: (layer, i, 0, 0, 0, 0))
    rows = lambda x: x.reshape(B // bs, bs, A_WIDTH)
    out = pl.pallas_call(
        functools.partial(_attn_sample_kernel, bs=bs),
        out_shape=jax.ShapeDtypeStruct((B // bs, bs, G_WIDTH), F32),
        grid=(B // bs,),
        in_specs=[vec, vec, vec] + [cache_spec(c) for c in caches_t],
        out_specs=pl.BlockSpec((1, bs, G_WIDTH), lambda i: (i, 0, 0)),
        compiler_params=_cparams(("parallel",)),
        name="attn_sample",
    )(rows(q), rows(k_new), rows(v_new), *caches_t)
    return out.reshape(B, G_WIDTH)


def _cache_shift_kernel(new_ref, ca_ref, cb_ref, cc_ref, oa_ref, ob_ref, oc_ref, *, bs):
    rows = 2 * G_WIDTH
    for gi, (c_ref, o_ref) in enumerate(((ca_ref, oa_ref), (cb_ref, ob_ref), (cc_ref, oc_ref))):
        win = c_ref.shape[-1]
        last = _iota((rows, win), 1) == win - 1
        for n in range(bs):
            x = c_ref[0, n].reshape(rows, win)
            col = new_ref[0, gi * rows:(gi + 1) * rows, n:n + 1]
            o_ref[0, n] = jnp.where(last, col, pltpu.roll(x, win - 1, 1)).reshape(o_ref.shape[2:])


def _cache_shift(caches_t, new_cols, *, bs=2):
    depth, B = caches_t[0].shape[:2]
    assert B % bs == 0
    nb = B // bs
    spec = lambda c: pl.BlockSpec((1, bs) + c.shape[2:], lambda l, i: (l, i, 0, 0, 0, 0))
    n_rows = new_cols.shape[1]
    new_b = new_cols.reshape(depth, n_rows, nb, bs).transpose(0, 2, 1, 3).reshape(depth * nb, n_rows, bs)
    return pl.pallas_call(
        functools.partial(_cache_shift_kernel, bs=bs),
        out_shape=tuple(jax.ShapeDtypeStruct(c.shape, c.dtype) for c in caches_t),
        grid=(depth, nb),
        in_specs=[pl.BlockSpec((1, n_rows, bs), lambda l, i: (l * nb + i, 0, 0))] + [spec(c) for c in caches_t],
        out_specs=tuple(spec(c) for c in caches_t),
        compiler_params=_cparams(("parallel", "parallel")),
        name="cache_shift",
    )(new_b, *caches_t)


def _merge_kernel(gt_ref, yr_ref, yc_ref, ao_ref, x_ref, wa_ref, wo_ref, g_ref, b_ref, h_ref, *, alpha):
    d = x_ref.shape[1]
    gate = lambda i: gt_ref[:, i * d:(i + 1) * d].astype(F32)
    y_a = _dot(ao_ref[...], wa_ref[...])
    merged = gate(0) * yr_ref[...].astype(F32) + gate(1) * yc_ref[...].astype(F32) + gate(2) * y_a
    h_ref[...] = _layer_norm(alpha * x_ref[...] + _dot(merged, wo_ref[...]), g_ref[...], b_ref[...])


def _merge(gates, y_r, y_c, a_o, x, prm, *, alpha, t=512):
    M, D = x.shape
    t = min(t, M)
    assert M % t == 0
    row = lambda a: a.reshape(1, -1)
    tile = lambda a: pl.BlockSpec((t, a.shape[1]), lambda i: (i, 0))
    full = lambda a: pl.BlockSpec(a.shape, lambda i: (0, 0))
    acts = [gates, y_r, y_c, a_o, x]
    params = [prm["w_attn_out_bf"], prm["w_o_bf"], row(prm["ln1_g"]), row(prm["ln1_b"])]
    return pl.pallas_call(
        functools.partial(_merge_kernel, alpha=alpha),
        out_shape=jax.ShapeDtypeStruct((M, D), F32),
        grid=(M // t,),
        in_specs=[tile(a) for a in acts] + [full(a) for a in params],
        out_specs=pl.BlockSpec((t, D), lambda i: (i, 0)),
        compiler_params=_cparams(("parallel",)),
        name="merge",
    )(*acts, *params)


def _ffn_kernel(*refs, alpha, t, tiles_per_seq, sample):
    if sample:
        (h_ref, pg_ref, pu_ref, wg_ref, wu_ref, dg_ref, du_ref, bg_ref, bu_ref, wo_ref, g_ref, b_ref,
         o_ref, ng_ref, nu_ref, acc_scr) = refs
    else:
        (h_ref, wg_ref, wu_ref, dg_ref, du_ref, bg_ref, bu_ref, wo_ref, g_ref, b_ref,
         o_ref, ng_ref, nu_ref, acc_scr, cg_scr, cu_scr) = refs
    i = pl.program_id(0)
    j = pl.program_id(1)
    nj = pl.num_programs(1)

    @pl.when(j == 0)
    def _():
        acc_scr[...] = jnp.zeros_like(acc_scr)

    def conv(u, dw_ref, bias_ref, p1, p2):
        return dw_ref[0:1, :] * p2 + dw_ref[1:2, :] * p1 + dw_ref[2:3, :] * u + bias_ref[...]

    if sample:
        hb = h_ref[...].astype(BF16)
        y = []
        for w_ref, dw_ref, bias_ref, new_ref, prev_ref in ((wg_ref, dg_ref, bg_ref, ng_ref, pg_ref),
                                                           (wu_ref, du_ref, bu_ref, nu_ref, pu_ref)):
            u = jnp.dot(hb, w_ref[...], preferred_element_type=F32)
            new_ref[...] = u
            y.append(conv(u, dw_ref, bias_ref, prev_ref[1], prev_ref[0]))
        acc_scr[...] += _dot((y[0] * _sigmoid(y[0])) * y[1], wo_ref[...])
    else:
        tp = t // FFN_ROW_PARTS
        fresh = (i % tiles_per_seq) == 0
        row = _iota((SUBLANES, wg_ref.shape[1]), 0)
        hb = [h_ref[s * tp:(s + 1) * tp, :].astype(BF16) for s in range(FFN_ROW_PARTS)]
        branches = ((wg_ref, dg_ref, bg_ref, ng_ref, cg_scr), (wu_ref, du_ref, bu_ref, nu_ref, cu_scr))
        u = [[jnp.dot(hb[s], w_ref[...], preferred_element_type=F32) for w_ref, *_ in branches]
             for s in range(FFN_ROW_PARTS)]
        for s in range(FFN_ROW_PARTS):
            y = []
            for bi, (_, dw_ref, bias_ref, new_ref, c_scr) in enumerate(branches):
                us = u[s][bi]
                if s == 0:
                    c1 = jnp.where(fresh, 0.0, c_scr[j, 7:8, :])
                    c2 = jnp.where(fresh, 0.0, c_scr[j, 6:7, :])
                else:
                    c1 = u[s - 1][bi][tp - 1:tp, :]
                    c2 = u[s - 1][bi][tp - 2:tp - 1, :]
                r1 = pltpu.roll(us, 1, 0)
                r2 = pltpu.roll(us, 2, 0)
                p1 = jnp.concatenate([jnp.where(row == 0, c1, r1[:SUBLANES]), r1[SUBLANES:]], axis=0)
                p2 = jnp.concatenate([jnp.where(row == 0, c2, jnp.where(row == 1, c1, r2[:SUBLANES])),
                                      r2[SUBLANES:]], axis=0)
                y.append(conv(us, dw_ref, bias_ref, p1, p2))
                if s == FFN_ROW_PARTS - 1:
                    new_ref[0] = us[tp - 2:tp, :]
                    c_scr[j, :, :] = us[tp - SUBLANES:tp, :]
            acc_scr[s * tp:(s + 1) * tp, :] += _dot((y[0] * _sigmoid(y[0])) * y[1], wo_ref[...])

    @pl.when(j == nj - 1)
    def _():
        o_ref[...] = _layer_norm(alpha * h_ref[...] + acc_scr[...], g_ref[...], b_ref[...])


def _ffn(h, prm, *, alpha, seq_len, prev=None, t=512, tn=2816):
    M, D = h.shape
    d_ff = prm["w_ffn_out_bf"].shape[0]
    sample = prev is not None
    t = M if sample else min(t, seq_len)
    tn = min(tn, d_ff)
    assert M % t == 0 and d_ff % tn == 0 and (sample or (seq_len % t == 0 and t >= 8))
    J = d_ff // tn
    row = lambda a: a.reshape(1, -1)
    w_in, dw, dwb = prm["w_ffn_in_bf"], prm["ffn_dw"], row(prm["ffn_dw_b"])
    col_g = lambda shape: pl.BlockSpec(shape, lambda i, j: (0, j))
    col_u = lambda shape: pl.BlockSpec(shape, lambda i, j: (0, J + j))
    const = lambda a: pl.BlockSpec(a.shape, lambda i, j: (0, 0))
    ins, specs = [h], [pl.BlockSpec((t, D), lambda i, j: (i, 0))]
    if sample:
        pstack = jnp.stack(prev)
        ins += [pstack, pstack]
        specs += [pl.BlockSpec((2, M, tn), lambda i, j: (0, 0, j)), pl.BlockSpec((2, M, tn), lambda i, j: (0, 0, J + j))]
    ins += [w_in, w_in, dw, dw, dwb, dwb, prm["w_ffn_out_bf"], row(prm["ln2_g"]), row(prm["ln2_b"])]
    specs += [col_g((D, tn)), col_u((D, tn)), col_g((FFN_K, tn)), col_u((FFN_K, tn)), col_g((1, tn)),
              col_u((1, tn)), pl.BlockSpec((tn, D), lambda i, j: (j, 0)), const(row(prm["ln2_g"])),
              const(row(prm["ln2_b"]))]
    if sample:
        new_shape = jax.ShapeDtypeStruct((M, d_ff), F32)
        new_spec = pl.BlockSpec((M, tn), lambda i, j: (0, j))
        scratch = [pltpu.VMEM((t, D), F32)]
        tiles_per_seq = 1
    else:
        tiles_per_seq = seq_len // t
        new_shape = jax.ShapeDtypeStruct((M // t, 2, d_ff), F32)
        new_spec = pl.BlockSpec((1, 2, tn), lambda i, j: (i, 0, j))
        scratch = [pltpu.VMEM((t, D), F32), pltpu.VMEM((J, 8, tn), F32), pltpu.VMEM((J, 8, tn), F32)]
    out, new_g, new_u = pl.pallas_call(
        functools.partial(_ffn_kernel, alpha=alpha, t=t, tiles_per_seq=tiles_per_seq, sample=sample),
        out_shape=(jax.ShapeDtypeStruct((M, D), F32), new_shape, new_shape),
        grid=(M // t, J),
        in_specs=specs,
        out_specs=(pl.BlockSpec((t, D), lambda i, j: (i, 0)), new_spec, new_spec),
        scratch_shapes=scratch,
        compiler_params=_cparams(("arbitrary", "arbitrary")),
        name="ffn_sample" if sample else "ffn_prompt",
    )(*ins)
    if not sample:
        new_g = new_g[tiles_per_seq - 1::tiles_per_seq]
        new_u = new_u[tiles_per_seq - 1::tiles_per_seq]
    return out, new_g, new_u


def _layer_weights(l, w_in, w_rwkv_out, w_conv_out, w_attn_out, w_o, w_ffn_in, w_ffn_out, small):
    prm = {k: v[l] for k, v in small.items()}
    wi = w_in[l]
    o_c = R_COLS
    o_q = o_c + 2 * CONV_CH
    o_g = o_q + 3 * A_WIDTH
    prm["w_in_bf"] = jnp.concatenate([wi[:, o_g:], wi[:, :o_c], wi[:, o_q:o_g], wi[:, o_c:o_q]], axis=1).astype(BF16)
    prm["w_rwkv_out_bf"] = w_rwkv_out[l].astype(BF16)
    prm["w_conv_out_bf"] = w_conv_out[l].astype(BF16)
    prm["w_attn_out_bf"] = w_attn_out[l].astype(BF16)
    prm["w_o_bf"] = w_o[l].astype(BF16)
    prm["w_ffn_in_bf"] = w_ffn_in[l].astype(BF16)
    prm["w_ffn_out_bf"] = w_ffn_out[l].astype(BF16)
    return prm


def _project_kernel(x_ref, w_ref, gb_ref, g_ref, z_ref, *, n_gate):
    j = pl.program_id(1)
    xw = lambda: jnp.dot(x_ref[...].astype(BF16), w_ref[...], preferred_element_type=F32)

    @pl.when(j < n_gate)
    def _():
        g_ref[...] = _sigmoid(xw() + gb_ref[...]).astype(g_ref.dtype)

    @pl.when(j >= n_gate)
    def _():
        z_ref[...] = xw()


def _project(x2, prm, tm, tn=1024):
    M, K = x2.shape
    w = prm["w_in_bf"]
    gate_w = prm["gate_b"].size
    tm = min(tm, M)
    assert M % tm == 0 and gate_w % tn == 0 and Z_WIDTH % tn == 0 and w.shape[1] == gate_w + Z_WIDTH
    n_gate = gate_w // tn
    return pl.pallas_call(
        functools.partial(_project_kernel, n_gate=n_gate),
        out_shape=(jax.ShapeDtypeStruct((M, gate_w), BF16), jax.ShapeDtypeStruct((M, Z_WIDTH), F32)),
        grid=(M // tm, w.shape[1] // tn),
        in_specs=[pl.BlockSpec((tm, K), lambda i, j: (i, 0)),
                  pl.BlockSpec((K, tn), lambda i, j: (0, j)),
                  pl.BlockSpec((1, tn), lambda i, j: (0, jnp.minimum(j, n_gate - 1)))],
        out_specs=(pl.BlockSpec((tm, tn), lambda i, j: (i, jnp.minimum(j, n_gate - 1))),
                   pl.BlockSpec((tm, tn), lambda i, j: (i, jnp.maximum(j - n_gate, 0)))),
        compiler_params=_cparams(("parallel", "arbitrary")),
        name="project",
    )(x2, w, prm["gate_b"].reshape(1, -1))


def _kv_rows(z_qkv, gi):
    k = z_qkv[..., A_WIDTH + gi * G_WIDTH:A_WIDTH + (gi + 1) * G_WIDTH]
    v = z_qkv[..., 2 * A_WIDTH + gi * G_WIDTH:2 * A_WIDTH + (gi + 1) * G_WIDTH]
    kv = jnp.stack([k, v], axis=-2)
    return kv.reshape(kv.shape[:-1] + (G_HEADS, HEAD_DIM))


def _prompt_layer(x, prm, alpha):
    B, L, D = x.shape
    x2 = x.reshape(B * L, D)
    gates, z = _project(x2, prm, tm=1024)
    z = z.reshape(B, L, Z_WIDTH)
    y_r, new_wkv = _rwkv_prompt(z, prm)
    y_c, new_conv = _conv_prompt(z, prm)
    a_o = _attn_prompt(z)
    h = _merge(gates, y_r.reshape(B * L, D), y_c.reshape(B * L, D), a_o.reshape(B * L, -1), x2, prm, alpha=alpha)
    out, nf_g, nf_u = _ffn(h, prm, alpha=alpha, seq_len=L)
    new_kv = [_kv_rows(z[:, L - min(win, L):, Z_QKV0:Z_C0], gi) for gi, (win, _) in enumerate(SWA_GROUPS)]
    return (out.reshape(B, L, D), z[:, L - 1:, :R_COLS], new_wkv, new_conv, new_kv,
            jnp.concatenate([nf_g, nf_u], axis=-1))


def _sample_layer(x, prm, alpha, shift, wkv, conv_buf, ffn_buf, caches_t, layer):
    B, L, D = x.shape
    assert L == 1
    x2 = x.reshape(B, D)
    z_g, z = _project(x2, prm, tm=B)
    z_r, z_qkv, z_c = z[:, :Z_QKV0], z[:, Z_QKV0:Z_C0], z[:, Z_C0:]
    o_r, new_wkv = _rwkv_sample(z_r, shift.reshape(B, R_COLS), wkv, prm)
    y_r = _mm(o_r, prm["w_rwkv_out_bf"], tm=B, tn=1024, out_dtype=BF16)
    c, u = _conv_sample(z_c, conv_buf, prm)
    y_c = _mm(c, prm["w_conv_out_bf"], tm=B, tn=1024, out_dtype=BF16)
    a_o = _attn_sample(z_qkv[:, :A_WIDTH], z_qkv[:, A_WIDTH:2 * A_WIDTH], z_qkv[:, 2 * A_WIDTH:], caches_t, layer)
    h = _merge(z_g, y_r, y_c, a_o, x2, prm, alpha=alpha)
    out, u_g, u_u = _ffn(h, prm, alpha=alpha, seq_len=1, prev=(ffn_buf[:, 0], ffn_buf[:, 1]))
    kv_new = []
    for gi in range(N_GROUPS):
        kv_new += [z_qkv[:, A_WIDTH + gi * G_WIDTH:A_WIDTH + (gi + 1) * G_WIDTH],
                   z_qkv[:, 2 * A_WIDTH + gi * G_WIDTH:2 * A_WIDTH + (gi + 1) * G_WIDTH]]
    return (out.reshape(B, 1, D), z_r.reshape(B, 1, R_COLS), new_wkv, u, jnp.concatenate(kv_new, axis=-1),
            jnp.concatenate([u_g, u_u], axis=-1))


def _append_row(buf, rows):
    return jnp.concatenate([buf[:, :, 1:], rows[:, :, None]], axis=2)


def kernel(x_prompt, x_sample, state_shift, state_wkv, state_conv, cache_swa_a, cache_swa_b, cache_swa_c, state_ffn, w_in, rwkv_mu, rwkv_w0, rwkv_w_up, rwkv_a0, rwkv_a_up, rwkv_g_up, rwkv_k_k, rwkv_k_a, rwkv_r_k, rwkv_ln_g, rwkv_ln_b, w_rwkv_out, conv_dw, conv_dw_b, conv_ln_g, conv_ln_b, w_conv_out, w_attn_out, gate_b, w_o, ln1_g, ln1_b, w_ffn_in, ffn_dw, ffn_dw_b, w_ffn_out, ln2_g, ln2_b):
    depth = w_in.shape[0]
    alpha = (2 * depth) ** 0.25
    small = dict(rwkv_mu=rwkv_mu, rwkv_w0=rwkv_w0, rwkv_w_up=rwkv_w_up, rwkv_a0=rwkv_a0, rwkv_a_up=rwkv_a_up,
                 rwkv_g_up=rwkv_g_up, rwkv_k_k=rwkv_k_k, rwkv_k_a=rwkv_k_a,
                 rwkv_r_k=rwkv_r_k.reshape(depth, -1), rwkv_ln_g=rwkv_ln_g, rwkv_ln_b=rwkv_ln_b,
                 conv_dw=conv_dw, conv_dw_b=conv_dw_b, conv_ln_g=conv_ln_g, conv_ln_b=conv_ln_b,
                 gate_b=gate_b.reshape(depth, -1), ln1_g=ln1_g, ln1_b=ln1_b, ffn_dw=ffn_dw, ffn_dw_b=ffn_dw_b,
                 ln2_g=ln2_g, ln2_b=ln2_b)
    caches = (cache_swa_a, cache_swa_b, cache_swa_c)
    caches_t = [jnp.transpose(c, (0, 1, 3, 4, 5, 2)) for c in caches]
    hp, hs = x_prompt, x_sample
    outs_p, outs_s = [], []
    for l in range(depth):
        prm = _layer_weights(l, w_in, w_rwkv_out, w_conv_out, w_attn_out, w_o, w_ffn_in, w_ffn_out, small)
        res_p = _prompt_layer(hp, prm, alpha)
        hp = res_p[0]
        outs_p.append(res_p[1:])
        res_s = _sample_layer(hs, prm, alpha, state_shift[l], state_wkv[l], state_conv[l], state_ffn[l],
                              caches_t, l)
        hs = res_s[0]
        outs_s.append(res_s[1:])
    stk = lambda outs, f: jnp.stack([f(o) for o in outs])
    res = [hp, hs]
    res += [stk(outs_p, lambda o: o[0]), stk(outs_s, lambda o: o[0])]
    res += [stk(outs_p, lambda o: o[1]), stk(outs_s, lambda o: o[1])]
    res += [stk(outs_p, lambda o: o[2]), _append_row(state_conv, stk(outs_s, lambda o: o[2]))]
    new_cols = jnp.transpose(stk(outs_s, lambda o: o[3]), (0, 2, 1))
    shifted = _cache_shift(caches_t, new_cols)
    for gi in range(N_GROUPS):
        res += [stk(outs_p, lambda o: o[3][gi]), jnp.transpose(shifted[gi], (0, 1, 5, 2, 3, 4))]
    res += [stk(outs_p, lambda o: o[4]), _append_row(state_ffn, stk(outs_s, lambda o: o[4]))]
    return tuple(res)
```
